```python
import math
import jax, jax.numpy as jnp
from jax import lax
import numpy as np


D_MODEL = 4096
BATCH = 8
SEQ = 2048
DEPTH = 4

HEAD_DIM = 128
A_WIDTH = D_MODEL // 2
B_WIDTH = D_MODEL - A_WIDTH
DIFF_HEADS = A_WIDTH // (2 * HEAD_DIM)
DIL_HEADS = B_WIDTH // HEAD_DIM
DIL_PATTERNS = ((128, 1), (512, 4), (2048, 16))
EVEN_IN = 3 * A_WIDTH + 3 * B_WIDTH
WIN_Q_HEADS = D_MODEL // HEAD_DIM
WIN_KV_HEADS = 8
WIN_HALF = 128
ODD_IN = (WIN_Q_HEADS + 2 * WIN_KV_HEADS) * HEAD_DIM
FFN_HIDDEN = -(-8 * D_MODEL // (3 * 256)) * 256
Q_BLOCK = 128
ROPE_THETA = 10000.0
EPS = 1e-6
NEG_INF = -1e30
N_EVEN = (DEPTH + 1) // 2
N_ODD = DEPTH // 2

kernel_name = 'hybrid_diff_dilated_window_encoder'


def rms_norm(x, g):
    xf = x.astype(jnp.float32)
    y = xf * lax.rsqrt(jnp.mean(xf * xf, axis=-1, keepdims=True) + EPS)
    return (y * g.astype(jnp.float32)).astype(x.dtype)


def rope(x, pos):
    d = x.shape[-1]
    inv_freq = ROPE_THETA ** (-jnp.arange(0, d, 2, dtype=jnp.float32) / d)
    ang = pos.astype(jnp.float32)[:, None] * inv_freq[None, :]
    ang = jnp.concatenate([ang, ang], axis=-1)
    bshape = (1, x.shape[1]) + (1,) * (x.ndim - 3) + (d,)
    cos = jnp.cos(ang).reshape(bshape)
    sin = jnp.sin(ang).reshape(bshape)
    xf = x.astype(jnp.float32)
    rot = jnp.concatenate([-xf[..., d // 2:], xf[..., :d // 2]], axis=-1)
    return (xf * cos + rot * sin).astype(x.dtype)


def banded_attention(q, k, v, half_width, sink=None):
    b, n, h, d = q.shape
    hk = k.shape[2]
    g = h // hk
    blk = half_width
    nb = -(-n // blk)
    n_pad = nb * blk - n
    qp = jnp.pad(q, ((0, 0), (0, n_pad), (0, 0), (0, 0))).reshape(b, nb, blk, hk, g, d)
    kv_pad = ((0, 0), (blk, n_pad + blk), (0, 0), (0, 0))
    kb = jnp.pad(k, kv_pad).reshape(b, nb + 2, blk, hk, d)
    vb = jnp.pad(v, kv_pad).reshape(b, nb + 2, blk, hk, d)
    kw = jnp.concatenate([kb[:, :-2], kb[:, 1:-1], kb[:, 2:]], axis=2)
    vw = jnp.concatenate([vb[:, :-2], vb[:, 1:-1], vb[:, 2:]], axis=2)
    s = jnp.einsum('bnqhgd,bnkhd->bnhgqk', qp, kw, preferred_element_type=jnp.float32) * d ** -0.5
    qpos = jnp.arange(nb)[:, None] * blk + jnp.arange(blk)[None, :]
    kpos = jnp.arange(nb)[:, None] * blk - blk + jnp.arange(3 * blk)[None, :]
    rel = kpos[:, None, :] - qpos[:, :, None]
    valid = (jnp.abs(rel) <= half_width) & (kpos[:, None, :] >= 0) & (kpos[:, None, :] < n)
    s = jnp.where(valid[None, :, None, None], s, NEG_INF)
    m = jnp.max(s, axis=-1)
    if sink is not None:
        sk = sink.astype(jnp.float32).reshape(1, 1, hk, g, 1)
        m = jnp.maximum(m, sk)
        denom = jnp.sum(jnp.exp(s - m[..., None]), axis=-1) + jnp.exp(sk - m)
    else:
        denom = jnp.sum(jnp.exp(s - m[..., None]), axis=-1)
    lse = m + jnp.log(denom)
    p = jnp.exp(s - lse[..., None]).astype(v.dtype)
    o = jnp.einsum('bnhgqk,bnkhd->bnqhgd', p, vw).reshape(b, nb * blk, h, d)[:, :n]
    lse = lse.transpose(0, 1, 4, 2, 3).reshape(b, nb * blk, h)[:, :n]
    return o, lse


def diff_attention(q, k, v, lam):
    b, s_len, h, _, d = q.shape
    nq = s_len // Q_BLOCK
    qb = q.reshape(b, nq, Q_BLOCK, h, 2, d).transpose(1, 0, 2, 3, 4, 5)
    scale = d ** -0.5

    def one_block(qblk):
        s = jnp.einsum('bqhtd,bkhtd->bhtqk', qblk, k, preferred_element_type=jnp.float32) * scale
        p = jax.nn.softmax(s, axis=-1)
        a = p[:, :, 0] - lam * p[:, :, 1]
        return jnp.einsum('bhqk,bkhe->bqhe', a.astype(v.dtype), v)

    out = lax.map(one_block, qb)
    return out.transpose(1, 0, 2, 3, 4).reshape(b, s_len, h, 2 * d)


def dilated_attention(q, k, v):
    b, s_len, h, d = q.shape
    outs, lses = [], []
    for window, dil in DIL_PATTERNS:
        sub = s_len // dil

        def fold(t):
            return t.reshape(b, sub, dil, h, d).transpose(0, 2, 1, 3, 4).reshape(b * dil, sub, h, d)

        o, lse = banded_attention(fold(q), fold(k), fold(v), window // (2 * dil))
        outs.append(o.reshape(b, dil, sub, h, d).transpose(0, 2, 1, 3, 4).reshape(b, s_len, h, d))
        lses.append(lse.reshape(b, dil, sub, h).transpose(0, 2, 1, 3).reshape(b, s_len, h))
    w = jax.nn.softmax(jnp.stack(lses), axis=0)
    return jnp.einsum('pbsh,pbshd->bshd', w.astype(q.dtype), jnp.stack(outs))


def even_mixer(h, pos, layer, w_in, w_out, q_norm_a, k_norm_a, lambdas, subln, q_norm_b, k_norm_b):
    b, s_len, _ = h.shape
    proj = h @ w_in
    aq, ak, av, bq, bk, bv = jnp.split(
        proj, [A_WIDTH, 2 * A_WIDTH, 3 * A_WIDTH, 3 * A_WIDTH + B_WIDTH, 3 * A_WIDTH + 2 * B_WIDTH], axis=-1)
    aq = rope(rms_norm(aq.reshape(b, s_len, DIFF_HEADS, 2, HEAD_DIM), q_norm_a), pos)
    ak = rope(rms_norm(ak.reshape(b, s_len, DIFF_HEADS, 2, HEAD_DIM), k_norm_a), pos)
    av = av.reshape(b, s_len, DIFF_HEADS, 2 * HEAD_DIM)
    lam_init = 0.8 - 0.6 * math.exp(-0.3 * layer)
    lf = lambdas.astype(jnp.float32)
    lam = jnp.exp(jnp.sum(lf[0] * lf[1])) - jnp.exp(jnp.sum(lf[2] * lf[3])) + lam_init
    ao = diff_attention(aq, ak, av, lam)
    ao = rms_norm(ao, subln) * (1.0 - lam_init)
    bq = rope(rms_norm(bq.reshape(b, s_len, DIL_HEADS, HEAD_DIM), q_norm_b), pos)
    bk = rope(rms_norm(bk.reshape(b, s_len, DIL_HEADS, HEAD_DIM), k_norm_b), pos)
    bv = bv.reshape(b, s_len, DIL_HEADS, HEAD_DIM)
    bo = dilated_attention(bq, bk, bv)
    cat = jnp.concatenate([ao.reshape(b, s_len, A_WIDTH), bo.reshape(b, s_len, B_WIDTH)], axis=-1)
    return cat @ w_out


def odd_mixer(h, pos, w_in, w_out, q_norm, k_norm, sink):
    b, s_len, _ = h.shape
    proj = h @ w_in
    qd = WIN_Q_HEADS * HEAD_DIM
    kd = WIN_KV_HEADS * HEAD_DIM
    q, k, v = jnp.split(proj, [qd, qd + kd], axis=-1)
    q = rope(rms_norm(q.reshape(b, s_len, WIN_Q_HEADS, HEAD_DIM), q_norm), pos)
    k = rope(rms_norm(k.reshape(b, s_len, WIN_KV_HEADS, HEAD_DIM), k_norm), pos)
    v = v.reshape(b, s_len, WIN_KV_HEADS, HEAD_DIM)
    o, _ = banded_attention(q, k, v, WIN_HALF, sink=sink)
    return o.reshape(b, s_len, qd) @ w_out


def swiglu(h, w_gate, w_up, w_down):
    return (jax.nn.silu(h @ w_gate) * (h @ w_up)) @ w_down


def setup_inputs(seed: int = 0) -> dict:
    key = jax.random.key(seed)
    ks = jax.random.split(key, 20)
    f32 = jnp.float32

    def dense(k, shape, fan_in):
        return jax.random.normal(k, shape, f32) * fan_in ** -0.5

    def gain(k, shape):
        return 1.0 + 0.02 * jax.random.normal(k, shape, f32)

    return {
        'x': jax.random.normal(ks[0], (BATCH, SEQ, D_MODEL), f32),
        'mix_norm': gain(ks[1], (DEPTH, D_MODEL)),
        'ffn_norm': gain(ks[2], (DEPTH, D_MODEL)),
        'w_gate': dense(ks[3], (DEPTH, D_MODEL, FFN_HIDDEN), D_MODEL),
        'w_up': dense(ks[4], (DEPTH, D_MODEL, FFN_HIDDEN), D_MODEL),
        'w_down': dense(ks[5], (DEPTH, FFN_HIDDEN, D_MODEL), FFN_HIDDEN),
        'hy_w_in': dense(ks[6], (N_EVEN, D_MODEL, EVEN_IN), D_MODEL),
        'hy_w_out': dense(ks[7], (N_EVEN, D_MODEL, D_MODEL), D_MODEL),
        'diff_q_norm': gain(ks[8], (N_EVEN, HEAD_DIM)),
        'diff_k_norm': gain(ks[9], (N_EVEN, HEAD_DIM)),
        'diff_lambda': 0.1 * jax.random.normal(ks[10], (N_EVEN, 4, HEAD_DIM), f32),
        'diff_subln': gain(ks[11], (N_EVEN, 2 * HEAD_DIM)),
        'dil_q_norm': gain(ks[12], (N_EVEN, HEAD_DIM)),
        'dil_k_norm': gain(ks[13], (N_EVEN, HEAD_DIM)),
        'win_w_in': dense(ks[14], (N_ODD, D_MODEL, ODD_IN), D_MODEL),
        'win_w_out': dense(ks[15], (N_ODD, D_MODEL, D_MODEL), D_MODEL),
        'win_q_norm': gain(ks[16], (N_ODD, HEAD_DIM)),
        'win_k_norm': gain(ks[17], (N_ODD, HEAD_DIM)),
        'win_sink': jax.random.normal(ks[18], (N_ODD, WIN_Q_HEADS), f32),
    }


def reference(x, mix_norm, ffn_norm, w_gate, w_up, w_down, hy_w_in, hy_w_out, diff_q_norm, diff_k_norm,
              diff_lambda, diff_subln, dil_q_norm, dil_k_norm, win_w_in, win_w_out, win_q_norm, win_k_norm,
              win_sink):
    pos = jnp.arange(x.shape[1])
    for layer in range(DEPTH):
        h = rms_norm(x, mix_norm[layer])
        if layer % 2 == 0:
            e = layer // 2
            mix = even_mixer(h, pos, layer, hy_w_in[e], hy_w_out[e], diff_q_norm[e], diff_k_norm[e],
                             diff_lambda[e], diff_subln[e], dil_q_norm[e], dil_k_norm[e])
        else:
            o = layer // 2
            mix = odd_mixer(h, pos, win_w_in[o], win_w_out[o], win_q_norm[o], win_k_norm[o], win_sink[o])
        x = x + mix
        h = rms_norm(x, ffn_norm[layer])
        x = x + swiglu(h, w_gate[layer], w_up[layer], w_down[layer])
    return x
```

```python
import functools
import math

import jax
import jax.numpy as jnp
from jax import lax
from jax.experimental import pallas as pl
from jax.experimental.pallas import tpu as pltpu

HEAD_DIM = 128
DIL_PATTERNS = ((128, 1), (512, 4), (2048, 16))
WIN_HALF = 128
ROPE_THETA = 10000.0
EPS = 1e-6
NEG_INF = -1e30

V7X_LANES = 128
V7X_VMEM_LIMIT_CAP = 56 * 1024 * 1024

F32 = jnp.float32
BF16 = jnp.bfloat16
NT_DIMS = (((1,), (1,)), ((), ()))


def _params(n_grid_dims, vmem_bytes):
    return pltpu.CompilerParams(
        dimension_semantics=("arbitrary",) * n_grid_dims,
        vmem_limit_bytes=int(min(vmem_bytes, V7X_VMEM_LIMIT_CAP)))


def _largest_tile(n, cap, quantum=V7X_LANES):
    best = None
    t = quantum
    while t <= min(n, cap):
        if n % t == 0:
            best = t
        t += quantum
    assert best is not None, (n, cap)
    return best


def _rmsnorm_kernel(x_ref, g_ref, o_ref):
    x = x_ref[...]
    ms = jnp.mean(x * x, axis=-1, keepdims=True)
    o_ref[...] = (x * lax.rsqrt(ms + EPS) * g_ref[...]).astype(o_ref.dtype)


def _rmsnorm(x, g):
    m, d = x.shape
    tm = _largest_tile(m, 256, 8)
    return pl.pallas_call(
        _rmsnorm_kernel,
        out_shape=jax.ShapeDtypeStruct((m, d), BF16),
        grid=(m // tm,),
        in_specs=[pl.BlockSpec((tm, d), lambda i: (i, 0)),
                  pl.BlockSpec((1, d), lambda i: (0, 0))],
        out_specs=pl.BlockSpec((tm, d), lambda i: (i, 0)),
        compiler_params=_params(1, 2 * tm * d * (4 + 2) + 4 * tm * d * 4),
        name="rmsnorm",
    )(x, g.reshape(1, d))


def _norm_rope(x, g, cos, sin):
    ms = jnp.mean(x * x, axis=-1, keepdims=True)
    y = x * lax.rsqrt(ms + EPS) * g
    return y * cos + pltpu.roll(y, HEAD_DIM // 2, axis=1) * sin


def _inproj_kernel(flag_ref, a_ref, b_ref, g_ref, cos_ref, sin_ref, o_ref):
    j = pl.program_id(1)
    acc = jnp.dot(a_ref[...], b_ref[...], preferred_element_type=F32)

    @pl.when(flag_ref[j] == 1)
    def _():
        cos = cos_ref[...]
        sin = sin_ref[...]
        for c in range(acc.shape[1] // HEAD_DIM):
            sl = slice(c * HEAD_DIM, (c + 1) * HEAD_DIM)
            o_ref[:, sl] = _norm_rope(acc[:, sl], g_ref[:, sl], cos, sin).astype(o_ref.dtype)

    @pl.when(flag_ref[j] == 0)
    def _():
        o_ref[...] = acc.astype(o_ref.dtype)


def _inproj(h, w, gain_cols, flags, cos, sin, tn):
    m, k = h.shape
    n = w.shape[1]
    s = cos.shape[0]
    tm = _largest_tile(s, 1024, 8)
    n_pos_blocks = s // tm
    vmem = 2 * (tm * k * 2 + k * tn * 2 + tm * tn * 2 + 2 * tm * HEAD_DIM * 4) + 3 * tm * tn * 4
    return pl.pallas_call(
        _inproj_kernel,
        out_shape=jax.ShapeDtypeStruct((m, n), BF16),
        grid_spec=pltpu.PrefetchScalarGridSpec(
            num_scalar_prefetch=1,
            grid=(m // tm, n // tn),
            in_specs=[pl.BlockSpec((tm, k), lambda i, j, f: (i, 0)),
                      pl.BlockSpec((k, tn), lambda i, j, f: (0, j)),
                      pl.BlockSpec((1, tn), lambda i, j, f: (0, j)),
                      pl.BlockSpec((tm, HEAD_DIM), lambda i, j, f: (i % n_pos_blocks, 0)),
                      pl.BlockSpec((tm, HEAD_DIM), lambda i, j, f: (i % n_pos_blocks, 0))],
            out_specs=pl.BlockSpec((tm, tn), lambda i, j, f: (i, j))),
        compiler_params=_params(2, vmem),
        name="inproj",
    )(flags, h, w, gain_cols, cos, sin)


def _mm_res_kernel(a_ref, b_ref, r_ref, o_ref):
    o_ref[...] = r_ref[...] + jnp.dot(a_ref[...], b_ref[...], preferred_element_type=F32)


def _mm_res(a, b, r):
    m, k = a.shape
    n = b.shape[1]
    tm = _largest_tile(m, 1024, 8)
    tn = _largest_tile(n, 1024)
    vmem = 2 * (tm * k * 2 + k * tn * 2 + 2 * tm * tn * 4) + 2 * tm * tn * 4
    return pl.pallas_call(
        _mm_res_kernel,
        out_shape=jax.ShapeDtypeStruct((m, n), F32),
        grid=(m // tm, n // tn),
        in_specs=[pl.BlockSpec((tm, k), lambda i, j: (i, 0)),
                  pl.BlockSpec((k, tn), lambda i, j: (0, j)),
                  pl.BlockSpec((tm, tn), lambda i, j: (i, j))],
        out_specs=pl.BlockSpec((tm, tn), lambda i, j: (i, j)),
        compiler_params=_params(2, vmem),
        name="outproj",
    )(a, b, r)


def _mm2_res_kernel(a1_ref, b1_ref, a2_ref, b2_ref, r_ref, o_ref):
    acc = jnp.dot(a1_ref[...], b1_ref[...], preferred_element_type=F32)
    acc += jnp.dot(a2_ref[...], b2_ref[...], preferred_element_type=F32)
    o_ref[...] = r_ref[...] + acc


def _mm2_res(a1, b1, a2, b2, r):
    m, k1 = a1.shape
    k2 = a2.shape[1]
    n = b1.shape[1]
    tm = _largest_tile(m, 1024, 8)
    tn = _largest_tile(n, 1024)
    vmem = 2 * (tm * (k1 + k2) * 2 + (k1 + k2) * tn * 2 + 2 * tm * tn * 4) + 2 * tm * tn * 4
    return pl.pallas_call(
        _mm2_res_kernel,
        out_shape=jax.ShapeDtypeStruct((m, n), F32),
        grid=(m // tm, n // tn),
        in_specs=[pl.BlockSpec((tm, k1), lambda i, j: (i, 0)),
                  pl.BlockSpec((k1, tn), lambda i, j: (0, j)),
                  pl.BlockSpec((tm, k2), lambda i, j: (i, 0)),
                  pl.BlockSpec((k2, tn), lambda i, j: (0, j)),
                  pl.BlockSpec((tm, tn), lambda i, j: (i, j))],
        out_specs=pl.BlockSpec((tm, tn), lambda i, j: (i, j)),
        compiler_params=_params(2, vmem),
        name="outproj2",
    )(a1, b1, a2, b2, r)


def _gateup_kernel(a_ref, wg_ref, wu_ref, o_ref):
    a = a_ref[...]
    g = jnp.dot(a, wg_ref[...], preferred_element_type=F32)
    u = jnp.dot(a, wu_ref[...], preferred_element_type=F32)
    o_ref[...] = (g * jax.nn.sigmoid(g) * u).astype(o_ref.dtype)


def _gateup(h, wg, wu):
    m, k = h.shape
    n = wg.shape[1]
    tm = _largest_tile(m, 1024, 8)
    tn = _largest_tile(n, 512)
    vmem = 2 * (tm * k * 2 + 2 * k * tn * 2 + tm * tn * 2) + 6 * tm * tn * 4
    return pl.pallas_call(
        _gateup_kernel,
        out_shape=jax.ShapeDtypeStruct((m, n), BF16),
        grid=(m // tm, n // tn),
        in_specs=[pl.BlockSpec((tm, k), lambda i, j: (i, 0)),
                  pl.BlockSpec((k, tn), lambda i, j: (0, j)),
                  pl.BlockSpec((k, tn), lambda i, j: (0, j))],
        out_specs=pl.BlockSpec((tm, tn), lambda i, j: (i, j)),
        compiler_params=_params(2, vmem),
        name="gateup",
    )(h, wg, wu)


def _down_kernel(a_ref, b_ref, r_ref, o_ref, acc_ref):
    kk = pl.program_id(2)

    @pl.when(kk == 0)
    def _():
        acc_ref[...] = r_ref[...]

    acc_ref[...] += jnp.dot(a_ref[...], b_ref[...], preferred_element_type=F32)

    @pl.when(kk == pl.num_programs(2) - 1)
    def _():
        o_ref[...] = acc_ref[...]


def _down(a, b, r, n_k_steps):
    m, k = a.shape
    n = b.shape[1]
    tm = _largest_tile(m, 1024, 8)
    tn = _largest_tile(n, 1024)
    assert k % (n_k_steps * V7X_LANES) == 0, (k, n_k_steps)
    tk = k // n_k_steps
    vmem = 2 * (tm * tk * 2 + tk * tn * 2 + 2 * tm * tn * 4) + 3 * tm * tn * 4
    return pl.pallas_call(
        _down_kernel,
        out_shape=jax.ShapeDtypeStruct((m, n), F32),
        grid=(m // tm, n // tn, n_k_steps),
        in_specs=[pl.BlockSpec((tm, tk), lambda i, j, kk: (i, kk)),
                  pl.BlockSpec((tk, tn), lambda i, j, kk: (kk, j)),
                  pl.BlockSpec((tm, tn), lambda i, j, kk: (i, j))],
        out_specs=pl.BlockSpec((tm, tn), lambda i, j, kk: (i, j)),
        scratch_shapes=[pltpu.VMEM((tm, tn), F32)],
        compiler_params=_params(3, vmem),
        name="down",
    )(a, b, r)


def _diff_attn_kernel(q_ref, k_ref, v_ref, lam_ref, subln_ref, o_ref, *, lam_init):
    q = q_ref[...]
    k = k_ref[...]
    v = v_ref[...]
    lf = lam_ref[...]
    lam = (jnp.exp(jnp.sum(lf[0:1] * lf[1:2], axis=-1, keepdims=True))
           - jnp.exp(jnp.sum(lf[2:3] * lf[3:4], axis=-1, keepdims=True)) + lam_init)

    def softmax_pv(qc, kc):
        s = lax.dot_general(qc, kc, NT_DIMS, preferred_element_type=F32)
        p = jnp.exp(s - jnp.max(s, axis=-1, keepdims=True))
        l = jnp.sum(p, axis=-1, keepdims=True)
        return jnp.dot(p.astype(BF16), v, preferred_element_type=F32) / l

    o = (softmax_pv(q[:, :HEAD_DIM], k[:, :HEAD_DIM])
         - lam * softmax_pv(q[:, HEAD_DIM:], k[:, HEAD_DIM:]))
    ms = jnp.mean(o * o, axis=-1, keepdims=True)
    y = o * lax.rsqrt(ms + EPS) * subln_ref[...]
    o_ref[...] = (y * (1.0 - lam_init)).astype(o_ref.dtype)


def _diff_attn(proj, lambdas, subln, batch, seq, a_width, lam_init):
    m = proj.shape[0]
    hd2 = 2 * HEAD_DIM
    n_heads = a_width // hd2
    tq = _largest_tile(seq, 512, 8)
    nq = seq // tq
    vmem = 2 * (2 * tq * hd2 * 2 + 2 * seq * hd2 * 2) + 6 * tq * seq * 4
    return pl.pallas_call(
        functools.partial(_diff_attn_kernel, lam_init=lam_init),
        out_shape=jax.ShapeDtypeStruct((m, a_width), BF16),
        grid=(batch, n_heads, nq),
        in_specs=[pl.BlockSpec((tq, hd2), lambda b, h, t: (b * nq + t, h)),
                  pl.BlockSpec((seq, hd2), lambda b, h, t: (b, n_heads + h)),
                  pl.BlockSpec((seq, hd2), lambda b, h, t: (b, 2 * n_heads + h)),
                  pl.BlockSpec((4, HEAD_DIM), lambda b, h, t: (0, 0)),
                  pl.BlockSpec((1, hd2), lambda b, h, t: (0, 0))],
        out_specs=pl.BlockSpec((tq, hd2), lambda b, h, t: (b * nq + t, h)),
        compiler_params=_params(3, vmem),
        name="diff_attn",
    )(proj, proj, proj, lambdas, subln.reshape(1, hd2))


BAND_TQ = 256


def _band_geometry(n, half_width):
    tq = min(BAND_TQ, n)
    kw = min(n, tq + 2 * half_width)
    return tq, kw, n // tq


def _band_key_start(q0, n, kw, half_width):
    return pl.multiple_of(jnp.clip(q0 - half_width, 0, n - kw), half_width)


def _band_tile(q, k, v, k0_minus_q0, half_width, sink=None):
    tq, kw = q.shape[0], k.shape[0]
    s = lax.dot_general(q, k, NT_DIMS, preferred_element_type=F32)
    rel = (lax.broadcasted_iota(jnp.int32, (tq, kw), 1) + k0_minus_q0
           - lax.broadcasted_iota(jnp.int32, (tq, kw), 0))
    s = jnp.where(jnp.abs(rel) <= half_width, s, NEG_INF)
    m = jnp.max(s, axis=-1, keepdims=True)
    if sink is not None:
        m = jnp.maximum(m, sink)
    p = jnp.exp(s - m)
    l = jnp.sum(p, axis=-1, keepdims=True)
    if sink is not None:
        l = l + jnp.exp(sink - m)
    o = jnp.dot(p.astype(BF16), v, preferred_element_type=F32) / l
    return o, m + jnp.log(l)


def _dilated_kernel(q_ref, k_ref, v_ref, o_ref, qf_ref, kf_ref, vf_ref, acc_ref, lse_ref):
    seq = q_ref.shape[0]
    qf_ref[...] = q_ref[...].astype(F32)
    kf_ref[...] = k_ref[...].astype(F32)
    vf_ref[...] = v_ref[...].astype(F32)

    strided = [(w, d) for (w, d) in DIL_PATTERNS if d > 1]
    for slot, (window, dil) in enumerate(strided):
        sub = seq // dil
        hw = window // (2 * dil)
        tq, kw, n_tiles = _band_geometry(sub, hw)

        def residue_body(r, carry, slot=slot, dil=dil, sub=sub, hw=hw, tq=tq, kw=kw, n_tiles=n_tiles):
            rows = pl.ds(r, sub, stride=dil)
            qs = qf_ref[rows, :].astype(BF16)
            ks = kf_ref[rows, :].astype(BF16)
            vs = vf_ref[rows, :].astype(BF16)
            for t in range(n_tiles):
                q0 = t * tq
                k0 = min(max(q0 - hw, 0), sub - kw)
                o, lse = _band_tile(qs[q0:q0 + tq], ks[k0:k0 + kw], vs[k0:k0 + kw], k0 - q0, hw)
                dst = pl.ds(r + dil * q0, tq, stride=dil)
                acc_ref[slot, dst, :] = o
                lse_ref[slot, dst, :] = jnp.broadcast_to(lse, (tq, HEAD_DIM))
            return carry

        lax.fori_loop(0, dil, residue_body, 0)

    (window, dil), = [(w, d) for (w, d) in DIL_PATTERNS if d == 1]
    hw = window // 2
    tq, kw, n_tiles = _band_geometry(seq, hw)

    def tile_body(t, carry):
        q0 = pl.multiple_of(t * tq, tq)
        k0 = _band_key_start(q0, seq, kw, hw)
        qrows, krows = pl.ds(q0, tq), pl.ds(k0, kw)
        o, lse = _band_tile(q_ref[qrows, :], k_ref[krows, :], v_ref[krows, :], k0 - q0, hw)
        lses = [lse] + [lse_ref[slot, qrows, :] for slot in range(len(strided))]
        outs = [o] + [acc_ref[slot, qrows, :] for slot in range(len(strided))]
        top = functools.reduce(jnp.maximum, lses)
        ws = [jnp.exp(x - top) for x in lses]
        num = functools.reduce(lambda a, b: a + b, [w * x for w, x in zip(ws, outs)])
        den = functools.reduce(lambda a, b: a + b, ws)
        o_ref[qrows, :] = (num / den).astype(o_ref.dtype)
        return carry

    lax.fori_loop(0, n_tiles, tile_body, 0)


def _dilated_attn(proj, batch, seq, col0, b_width):
    m = proj.shape[0]
    n_heads = b_width // HEAD_DIM
    c0 = col0 // HEAD_DIM
    n_strided = sum(1 for (_, d) in DIL_PATTERNS if d > 1)
    vmem = (2 * 4 * seq * HEAD_DIM * 2 + (3 + 2 * n_strided) * seq * HEAD_DIM * 4
            + 16 * 1024 * 1024)
    spec = lambda off: pl.BlockSpec((seq, HEAD_DIM), lambda b, h: (b, c0 + off + h))
    return pl.pallas_call(
        _dilated_kernel,
        out_shape=jax.ShapeDtypeStruct((m, b_width), BF16),
        grid=(batch, n_heads),
        in_specs=[spec(0), spec(n_heads), spec(2 * n_heads)],
        out_specs=pl.BlockSpec((seq, HEAD_DIM), lambda b, h: (b, h)),
        scratch_shapes=[pltpu.VMEM((seq, HEAD_DIM), F32)] * 3
                       + [pltpu.VMEM((n_strided, seq, HEAD_DIM), F32)] * 2,
        compiler_params=_params(2, vmem),
        name="dilated_attn",
    )(proj, proj, proj)


def _window_kernel(sink_ref, q_ref, k_ref, v_ref, o_ref, *, group):
    seq = k_ref.shape[0]
    hk = pl.program_id(1)
    tq, kw, n_tiles = _band_geometry(seq, WIN_HALF)

    def tile_body(t, carry):
        q0 = pl.multiple_of(t * tq, tq)
        k0 = _band_key_start(q0, seq, kw, WIN_HALF)
        qrows, krows = pl.ds(q0, tq), pl.ds(k0, kw)
        k = k_ref[krows, :]
        v = v_ref[krows, :]
        for g in range(group):
            cols = slice(g * HEAD_DIM, (g + 1) * HEAD_DIM)
            o, _ = _band_tile(q_ref[qrows, cols], k, v, k0 - q0, WIN_HALF, sink=sink_ref[hk * group + g])
            o_ref[qrows, cols] = o.astype(o_ref.dtype)
        return carry

    lax.fori_loop(0, n_tiles, tile_body, 0)


def _window_attn(proj, sink, batch, seq, n_q_heads, n_kv_heads):
    m = proj.shape[0]
    group = n_q_heads // n_kv_heads
    gw = group * HEAD_DIM
    vmem = 2 * (2 * seq * gw * 2 + 2 * seq * HEAD_DIM * 2) + 16 * 1024 * 1024
    return pl.pallas_call(
        functools.partial(_window_kernel, group=group),
        out_shape=jax.ShapeDtypeStruct((m, n_q_heads * HEAD_DIM), BF16),
        grid_spec=pltpu.PrefetchScalarGridSpec(
            num_scalar_prefetch=1,
            grid=(batch, n_kv_heads),
            in_specs=[pl.BlockSpec((seq, gw), lambda b, h, s: (b, h)),
                      pl.BlockSpec((seq, HEAD_DIM), lambda b, h, s: (b, n_q_heads + h)),
                      pl.BlockSpec((seq, HEAD_DIM), lambda b, h, s: (b, n_q_heads + n_kv_heads + h))],
            out_specs=pl.BlockSpec((seq, gw), lambda b, h, s: (b, h))),
        compiler_params=_params(2, vmem),
        name="window_attn",
    )(sink, proj, proj, proj)


def _rope_tables(seq):
    inv_freq = ROPE_THETA ** (-jnp.arange(0, HEAD_DIM, 2, dtype=F32) / HEAD_DIM)
    ang = jnp.arange(seq, dtype=F32)[:, None] * inv_freq[None, :]
    ang = jnp.concatenate([ang, ang], axis=-1)
    sign = jnp.where(jnp.arange(HEAD_DIM) < HEAD_DIM // 2, -1.0, 1.0).astype(F32)
    return jnp.cos(ang), jnp.sin(ang) * sign


def _section_gains(sections):
    cols, flags = [], []
    for width, g in sections:
        reps = width // HEAD_DIM
        if g is None:
            cols.append(jnp.ones((width,), F32))
            flags += [0] * reps
        else:
            cols.append(jnp.tile(g.astype(F32), reps))
            flags += [1] * reps
    return jnp.concatenate(cols).reshape(1, -1), flags


def _tile_flags(flags, tn):
    per = tn // HEAD_DIM
    tiles = [flags[i:i + per] for i in range(0, len(flags), per)]
    assert all(len(set(t)) == 1 for t in tiles), "a column tile mixes normed and plain heads"
    return jnp.asarray([t[0] for t in tiles], jnp.int32)


def _ffn(x, g, wg, wu, wd, f_pad, n_k_steps):
    pad = f_pad - wg.shape[1]
    wg = jnp.pad(wg.astype(BF16), ((0, 0), (0, pad)))
    wu = jnp.pad(wu.astype(BF16), ((0, 0), (0, pad)))
    wd = jnp.pad(wd.astype(BF16), ((0, pad), (0, 0)))
    h = _rmsnorm(x, g)
    hidden = _gateup(h, wg, wu)
    return _down(hidden, wd, x, n_k_steps)


def kernel(x, mix_norm, ffn_norm, w_gate, w_up, w_down, hy_w_in, hy_w_out, diff_q_norm, diff_k_norm,
           diff_lambda, diff_subln, dil_q_norm, dil_k_norm, win_w_in, win_w_out, win_q_norm, win_k_norm,
           win_sink):
    batch, seq, d_model = x.shape
    depth = mix_norm.shape[0]
    a_width = d_model // 2
    b_width = d_model - a_width
    n_q_heads = d_model // HEAD_DIM
    n_kv_heads = (win_w_in.shape[-1] // HEAD_DIM - n_q_heads) // 2
    scale = HEAD_DIM ** -0.5
    f_hidden = w_gate.shape[-1]
    ffn_k_steps = 4
    f_pad = -(-f_hidden // (ffn_k_steps * 2 * V7X_LANES)) * (ffn_k_steps * 2 * V7X_LANES)

    cos, sin = _rope_tables(seq)
    xf = x.reshape(batch * seq, d_model)

    for layer in range(depth):
        h = _rmsnorm(xf, mix_norm[layer])
        if layer % 2 == 0:
            e = layer // 2
            gains, flags = _section_gains([
                (a_width, diff_q_norm[e] * scale), (a_width, diff_k_norm[e]), (a_width, None),
                (b_width, dil_q_norm[e] * scale), (b_width, dil_k_norm[e]), (b_width, None)])
            tn = _largest_tile(math.gcd(a_width, b_width), 1024)
            proj = _inproj(h, hy_w_in[e].astype(BF16), gains, _tile_flags(flags, tn), cos, sin, tn)
            lam_init = 0.8 - 0.6 * math.exp(-0.3 * layer)
            ao = _diff_attn(proj, diff_lambda[e], diff_subln[e], batch, seq, a_width, lam_init)
            bo = _dilated_attn(proj, batch, seq, 3 * a_width, b_width)
            w_out = hy_w_out[e].astype(BF16)
            xf = _mm2_res(ao, w_out[:a_width], bo, w_out[a_width:], xf)
        else:
            o = layer // 2
            qd, kd = n_q_heads * HEAD_DIM, n_kv_heads * HEAD_DIM
            gains, flags = _section_gains([
                (qd, win_q_norm[o] * scale), (kd, win_k_norm[o]), (kd, None)])
            tn = _largest_tile(math.gcd(qd, kd), 1024)
            proj = _inproj(h, win_w_in[o].astype(BF16), gains, _tile_flags(flags, tn), cos, sin, tn)
            att = _window_attn(proj, win_sink[o].astype(F32), batch, seq, n_q_heads, n_kv_heads)
            xf = _mm_res(att, win_w_out[o].astype(BF16), xf)
        xf = _ffn(xf, ffn_norm[layer], w_gate[layer], w_up[layer], w_down[layer], f_pad, ffn_k_steps)
    return xf.reshape(batch, seq, d_model)
```

```python
import functools
import math

import jax
import jax.numpy as jnp
from jax import lax
from jax.experimental import pallas as pl
from jax.experimental.pallas import tpu as pltpu

HEAD_DIM = 128
DIL_PATTERNS = ((128, 1), (512, 4), (2048, 16))
WIN_HALF = 128
ROPE_THETA = 10000.0
EPS = 1e-6
NEG_INF = -1e30

V7X_LANES = 128
V7X_MXU_COLS = 256
V7X_VMEM_LIMIT_CAP = 56 * 1024 * 1024

F32 = jnp.float32
BF16 = jnp.bfloat16
NT_DIMS = (((1,), (1,)), ((), ()))


def _params(n_grid_dims, vmem_bytes):
    return pltpu.CompilerParams(
        dimension_semantics=("arbitrary",) * n_grid_dims,
        vmem_limit_bytes=int(min(vmem_bytes, V7X_VMEM_LIMIT_CAP)))


def _largest_tile(n, cap, quantum=V7X_LANES):
    best = None
    t = quantum
    while t <= min(n, cap):
        if n % t == 0:
            best = t
        t += quantum
    assert best is not None, (n, cap)
    return best


def _rmsnorm_kernel(x_ref, g_ref, o_ref):
    x = x_ref[...]
    ms = jnp.mean(x * x, axis=-1, keepdims=True)
    o_ref[...] = (x * lax.rsqrt(ms + EPS) * g_ref[...]).astype(o_ref.dtype)


def _rmsnorm(x, gains, layer):
    m, d = x.shape
    tm = _largest_tile(m, 256, 8)
    return pl.pallas_call(
        _rmsnorm_kernel,
        out_shape=jax.ShapeDtypeStruct((m, d), BF16),
        grid=(m // tm,),
        in_specs=[pl.BlockSpec((tm, d), lambda i: (i, 0)),
                  pl.BlockSpec((None, 1, d), lambda i: (layer, 0, 0))],
        out_specs=pl.BlockSpec((tm, d), lambda i: (i, 0)),
        compiler_params=_params(1, 2 * tm * d * (4 + 2) + 4 * tm * d * 4),
        name="rmsnorm",
    )(x, gains)


def _norm_rope(x, g, cos, sin):
    ms = jnp.mean(x * x, axis=-1, keepdims=True)
    y = x * lax.rsqrt(ms + EPS) * g
    return y * cos + pltpu.roll(y, HEAD_DIM // 2, axis=1) * sin


def _inproj_kernel(flag_ref, a_ref, b_ref, g_ref, cos_ref, sin_ref, o_ref):
    j = pl.program_id(1)
    acc = jnp.dot(a_ref[...], b_ref[...], preferred_element_type=F32)

    @pl.when(flag_ref[j] == 1)
    def _():
        cos = cos_ref[...]
        sin = sin_ref[...]
        for c in range(acc.shape[1] // HEAD_DIM):
            sl = slice(c * HEAD_DIM, (c + 1) * HEAD_DIM)
            o_ref[:, sl] = _norm_rope(acc[:, sl], g_ref[:, sl], cos, sin).astype(o_ref.dtype)

    @pl.when(flag_ref[j] == 0)
    def _():
        o_ref[...] = acc.astype(o_ref.dtype)


def _inproj(h, w, layer, gain_cols, flags, cos, sin, tn):
    m, k = h.shape
    n = w.shape[2]
    s = cos.shape[0]
    tm = _largest_tile(s, 1024, 8)
    n_pos_blocks = s // tm
    vmem = 2 * (tm * k * 2 + k * tn * 2 + tm * tn * 2 + 2 * tm * HEAD_DIM * 4) + 3 * tm * tn * 4
    return pl.pallas_call(
        _inproj_kernel,
        out_shape=jax.ShapeDtypeStruct((m, n), BF16),
        grid_spec=pltpu.PrefetchScalarGridSpec(
            num_scalar_prefetch=1,
            grid=(m // tm, n // tn),
            in_specs=[pl.BlockSpec((tm, k), lambda i, j, f: (i, 0)),
                      pl.BlockSpec((None, k, tn), lambda i, j, f: (layer, 0, j)),
                      pl.BlockSpec((1, tn), lambda i, j, f: (0, j)),
                      pl.BlockSpec((tm, HEAD_DIM), lambda i, j, f: (i % n_pos_blocks, 0)),
                      pl.BlockSpec((tm, HEAD_DIM), lambda i, j, f: (i % n_pos_blocks, 0))],
            out_specs=pl.BlockSpec((tm, tn), lambda i, j, f: (i, j))),
        compiler_params=_params(2, vmem),
        name="inproj",
    )(flags, h, w, gain_cols, cos, sin)


def _mm_res_kernel(a_ref, b_ref, r_ref, o_ref):
    o_ref[...] = r_ref[...] + jnp.dot(a_ref[...], b_ref[...], preferred_element_type=F32)


def _mm_res(a, w, layer, r):
    m, k = a.shape
    n = w.shape[2]
    tm = _largest_tile(m, 1024, 8)
    tn = _largest_tile(n, 1024)
    vmem = 2 * (tm * k * 2 + k * tn * 2 + 2 * tm * tn * 4) + 2 * tm * tn * 4
    return pl.pallas_call(
        _mm_res_kernel,
        out_shape=jax.ShapeDtypeStruct((m, n), F32),
        grid=(m // tm, n // tn),
        in_specs=[pl.BlockSpec((tm, k), lambda i, j: (i, 0)),
                  pl.BlockSpec((None, k, tn), lambda i, j: (layer, 0, j)),
                  pl.BlockSpec((tm, tn), lambda i, j: (i, j))],
        out_specs=pl.BlockSpec((tm, tn), lambda i, j: (i, j)),
        compiler_params=_params(2, vmem),
        name="outproj",
    )(a, w, r)


def _mm2_res_kernel(a1_ref, b1_ref, a2_ref, b2_ref, r_ref, o_ref):
    acc = jnp.dot(a1_ref[...], b1_ref[...], preferred_element_type=F32)
    acc += jnp.dot(a2_ref[...], b2_ref[...], preferred_element_type=F32)
    o_ref[...] = r_ref[...] + acc


def _mm2_res(a1, a2, w, layer, r):
    m, k1 = a1.shape
    k2 = a2.shape[1]
    assert k1 == k2, "the two head groups index w[layer] as row blocks 0 and 1"
    n = w.shape[2]
    tm = _largest_tile(m, 1024, 8)
    tn = _largest_tile(n, 1024)
    vmem = 2 * (tm * (k1 + k2) * 2 + (k1 + k2) * tn * 2 + 2 * tm * tn * 4) + 2 * tm * tn * 4
    return pl.pallas_call(
        _mm2_res_kernel,
        out_shape=jax.ShapeDtypeStruct((m, n), F32),
        grid=(m // tm, n // tn),
        in_specs=[pl.BlockSpec((tm, k1), lambda i, j: (i, 0)),
                  pl.BlockSpec((None, k1, tn), lambda i, j: (layer, 0, j)),
                  pl.BlockSpec((tm, k2), lambda i, j: (i, 0)),
                  pl.BlockSpec((None, k2, tn), lambda i, j: (layer, 1, j)),
                  pl.BlockSpec((tm, tn), lambda i, j: (i, j))],
        out_specs=pl.BlockSpec((tm, tn), lambda i, j: (i, j)),
        compiler_params=_params(2, vmem),
        name="outproj2",
    )(a1, w, a2, w, r)


def _gateup_kernel(a_ref, wg_ref, wu_ref, o_ref):
    a = a_ref[...]
    g = jnp.dot(a, wg_ref[...], preferred_element_type=F32)
    u = jnp.dot(a, wu_ref[...], preferred_element_type=F32)
    o_ref[...] = (g * jax.nn.sigmoid(g) * u).astype(o_ref.dtype)


def _gateup(h, wg, wu, layer):
    m, k = h.shape
    n = wg.shape[2]
    tm = _largest_tile(m, 1024, 8)
    tn = _largest_tile(n, 512, V7X_MXU_COLS)
    vmem = 2 * (tm * k * 2 + 2 * k * tn * 2 + tm * tn * 2) + 6 * tm * tn * 4
    return pl.pallas_call(
        _gateup_kernel,
        out_shape=jax.ShapeDtypeStruct((m, n), BF16),
        grid=(m // tm, n // tn),
        in_specs=[pl.BlockSpec((tm, k), lambda i, j: (i, 0)),
                  pl.BlockSpec((None, k, tn), lambda i, j: (layer, 0, j)),
                  pl.BlockSpec((None, k, tn), lambda i, j: (layer, 0, j))],
        out_specs=pl.BlockSpec((tm, tn), lambda i, j: (i, j)),
        compiler_params=_params(2, vmem),
        name="gateup",
    )(h, wg, wu)


def _down_kernel(a_ref, b_ref, r_ref, o_ref, acc_ref):
    kk = pl.program_id(2)

    @pl.when(kk == 0)
    def _():
        acc_ref[...] = r_ref[...]

    acc_ref[...] += jnp.dot(a_ref[...], b_ref[...], preferred_element_type=F32)

    @pl.when(kk == pl.num_programs(2) - 1)
    def _():
        o_ref[...] = acc_ref[...]


def _down(a, w, layer, r):
    m, k = a.shape
    n = w.shape[2]
    tm = _largest_tile(m, 1024, 8)
    tn = _largest_tile(n, 512)
    tk = _largest_tile(k, 6144)
    vmem = 2 * (tm * tk * 2 + tk * tn * 2 + 2 * tm * tn * 4) + 3 * tm * tn * 4
    return pl.pallas_call(
        _down_kernel,
        out_shape=jax.ShapeDtypeStruct((m, n), F32),
        grid=(m // tm, n // tn, k // tk),
        in_specs=[pl.BlockSpec((tm, tk), lambda i, j, kk: (i, kk)),
                  pl.BlockSpec((None, tk, tn), lambda i, j, kk: (layer, kk, j)),
                  pl.BlockSpec((tm, tn), lambda i, j, kk: (i, j))],
        out_specs=pl.BlockSpec((tm, tn), lambda i, j, kk: (i, j)),
        scratch_shapes=[pltpu.VMEM((tm, tn), F32)],
        compiler_params=_params(3, vmem),
        name="down",
    )(a, w, r)


def _diff_attn_kernel(q_ref, k_ref, v_ref, lam_ref, subln_ref, o_ref, *, lam_init):
    q = q_ref[...]
    k = k_ref[...]
    v = v_ref[...]
    lf = lam_ref[...]
    lam = (jnp.exp(jnp.sum(lf[0:1] * lf[1:2], axis=-1, keepdims=True))
           - jnp.exp(jnp.sum(lf[2:3] * lf[3:4], axis=-1, keepdims=True)) + lam_init)

    halves = (slice(0, HEAD_DIM), slice(HEAD_DIM, 2 * HEAD_DIM))
    scores = [lax.dot_general(q[:, c], k[:, c], NT_DIMS, preferred_element_type=F32) for c in halves]
    probs = []
    for s in scores:
        p = jnp.exp(s - jnp.max(s, axis=-1, keepdims=True))
        probs.append((p.astype(BF16), jnp.sum(p, axis=-1, keepdims=True)))
    o1, o2 = [jnp.dot(p, v, preferred_element_type=F32) / l for p, l in probs]
    o = o1 - lam * o2
    ms = jnp.mean(o * o, axis=-1, keepdims=True)
    y = o * lax.rsqrt(ms + EPS) * subln_ref[...]
    o_ref[...] = (y * (1.0 - lam_init)).astype(o_ref.dtype)


def _diff_attn(proj, lambdas, subln, layer, batch, seq, a_width, lam_init):
    m = proj.shape[0]
    hd2 = 2 * HEAD_DIM
    n_heads = a_width // hd2
    tq = _largest_tile(seq, 1024, 8)
    nq = seq // tq
    vmem = 2 * (2 * tq * hd2 * 2 + 2 * seq * hd2 * 2) + 6 * tq * seq * 4
    return pl.pallas_call(
        functools.partial(_diff_attn_kernel, lam_init=lam_init),
        out_shape=jax.ShapeDtypeStruct((m, a_width), BF16),
        grid=(batch, n_heads, nq),
        in_specs=[pl.BlockSpec((tq, hd2), lambda b, h, t: (b * nq + t, h)),
                  pl.BlockSpec((seq, hd2), lambda b, h, t: (b, n_heads + h)),
                  pl.BlockSpec((seq, hd2), lambda b, h, t: (b, 2 * n_heads + h)),
                  pl.BlockSpec((None, 4, HEAD_DIM), lambda b, h, t: (layer, 0, 0)),
                  pl.BlockSpec((None, 1, hd2), lambda b, h, t: (layer, 0, 0))],
        out_specs=pl.BlockSpec((tq, hd2), lambda b, h, t: (b * nq + t, h)),
        compiler_params=_params(3, vmem),
        name="diff_attn",
    )(proj, proj, proj, lambdas, subln)


BAND_TQ = 256


def _band_geometry(n, half_width):
    tq = min(BAND_TQ, n)
    kw = min(n, tq + 2 * half_width)
    return tq, kw, n // tq


def _band_key_start(q0, n, kw, half_width):
    return pl.multiple_of(jnp.clip(q0 - half_width, 0, n - kw), half_width)


def _band_mask(tq, kw, k0_minus_q0, half_width):
    rel = (lax.broadcasted_iota(jnp.int32, (tq, kw), 1) + k0_minus_q0
           - lax.broadcasted_iota(jnp.int32, (tq, kw), 0))
    return jnp.abs(rel) <= half_width


def _band_tiles(tiles, sinks=None):
    sinks = [None] * len(tiles) if sinks is None else sinks
    scores = [lax.dot_general(q, k, NT_DIMS, preferred_element_type=F32) for (q, k, _, _) in tiles]
    probs = []
    for s, (_, _, _, valid), sk in zip(scores, tiles, sinks):
        s = jnp.where(valid, s, NEG_INF)
        m = jnp.max(s, axis=-1, keepdims=True)
        if sk is not None:
            m = jnp.maximum(m, sk)
        p = jnp.exp(s - m)
        l = jnp.sum(p, axis=-1, keepdims=True)
        if sk is not None:
            l = l + jnp.exp(sk - m)
        probs.append((p.astype(BF16), m, l))
    return [(jnp.dot(p, v, preferred_element_type=F32) / l, m + jnp.log(l))
            for (p, m, l), (_, _, v, _) in zip(probs, tiles)]


def _fold(dst_ref, src_ref, factor, n_blocks, cast=None):
    n_tensors, rows = src_ref.shape[0], src_ref.shape[1]
    src_block = rows // n_blocks
    dst_block = src_block // factor
    for i in range(n_tensors):
        for j in range(n_blocks):
            for c in range(factor):
                v = src_ref[i, pl.ds(j * src_block + c, dst_block, stride=factor), :]
                dst_ref[i, (j * factor + c) * dst_block:(j * factor + c + 1) * dst_block, :] = (
                    v if cast is None else v.astype(cast))


def _unfold(dst_ref, src_ref, factor, n_blocks):
    n_tensors, rows = src_ref.shape[0], src_ref.shape[1]
    dst_block = rows // n_blocks
    src_block = dst_block // factor
    for i in range(n_tensors):
        for j in range(n_blocks):
            for c in range(factor):
                dst_ref[i, pl.ds(j * dst_block + c, src_block, stride=factor), :] = (
                    src_ref[i, (j * factor + c) * src_block:(j * factor + c + 1) * src_block, :])


def _dilated_kernel(q_ref, k_ref, v_ref, o_ref, nat_ref, x4_ref, x16_ref, r16_ref, u4_ref, r4_ref):
    seq = q_ref.shape[0]
    (w1, d1), (w4, d4), (w16, d16) = DIL_PATTERNS
    assert (d1, d4, d16) == (1, 4, 16) and w1 // (2 * d1) == w4 // (2 * d4) == w16 // (2 * d16)
    hw = w1 // 2

    for i, ref in enumerate((q_ref, k_ref, v_ref)):
        nat_ref[i] = ref[...].astype(F32)
    _fold(x4_ref, nat_ref, 4, 1)
    _fold(x16_ref, x4_ref, 4, 4, cast=BF16)

    sub16 = seq // 16
    tq, kw, n_tiles = _band_geometry(sub16, hw)
    assert n_tiles == 1 and kw == sub16
    valid16 = _band_mask(tq, kw, 0, hw)
    blocks = [slice(r * sub16, (r + 1) * sub16) for r in range(16)]
    tiles = [(x16_ref[0, rows, :], x16_ref[1, rows, :], x16_ref[2, rows, :], valid16) for rows in blocks]
    for rows, (o, lse) in zip(blocks, _band_tiles(tiles)):
        r16_ref[0, rows, :] = o
        r16_ref[1, rows, :] = jnp.broadcast_to(lse, (sub16, HEAD_DIM))
    _unfold(u4_ref, r16_ref, 4, 4)

    sub4 = seq // 4
    tq, kw, n_tiles = _band_geometry(sub4, hw)
    blocks, tiles = [], []
    for c in range(4):
        for t in range(n_tiles):
            q0 = t * tq
            k0 = min(max(q0 - hw, 0), sub4 - kw)
            qrows = slice(c * sub4 + q0, c * sub4 + q0 + tq)
            krows = slice(c * sub4 + k0, c * sub4 + k0 + kw)
            blocks.append(qrows)
            tiles.append((x4_ref[0, qrows, :].astype(BF16), x4_ref[1, krows, :].astype(BF16),
                          x4_ref[2, krows, :].astype(BF16), _band_mask(tq, kw, k0 - q0, hw)))
    for qrows, (o, lse) in zip(blocks, _band_tiles(tiles)):
        r4_ref[0, qrows, :] = o
        r4_ref[1, qrows, :] = jnp.broadcast_to(lse, (tq, HEAD_DIM))
    _unfold(r16_ref, u4_ref, 4, 1)
    _unfold(nat_ref, r4_ref, 4, 1)

    tq, kw, n_tiles = _band_geometry(seq, hw)
    blocks, tiles = [], []
    for t in range(n_tiles):
        q0 = t * tq
        k0 = min(max(q0 - hw, 0), seq - kw)
        qrows, krows = slice(q0, q0 + tq), slice(k0, k0 + kw)
        blocks.append(qrows)
        tiles.append((q_ref[qrows, :], k_ref[krows, :], v_ref[krows, :], _band_mask(tq, kw, k0 - q0, hw)))
    for qrows, (o, lse) in zip(blocks, _band_tiles(tiles)):
        lses = [lse, nat_ref[1, qrows, :], r16_ref[1, qrows, :]]
        outs = [o, nat_ref[0, qrows, :], r16_ref[0, qrows, :]]
        top = functools.reduce(jnp.maximum, lses)
        ws = [jnp.exp(x - top) for x in lses]
        num = functools.reduce(lambda a, b: a + b, [w * x for w, x in zip(ws, outs)])
        den = functools.reduce(lambda a, b: a + b, ws)
        o_ref[qrows, :] = (num / den).astype(o_ref.dtype)


def _dilated_attn(proj, batch, seq, col0, b_width):
    m = proj.shape[0]
    n_heads = b_width // HEAD_DIM
    c0 = col0 // HEAD_DIM
    plane = seq * HEAD_DIM
    scratch_bytes = (3 + 3 + 2 + 2 + 2) * plane * 4 + 3 * plane * 2
    vmem = 2 * 4 * plane * 2 + scratch_bytes + 12 * 1024 * 1024
    spec = lambda off: pl.BlockSpec((seq, HEAD_DIM), lambda b, h: (b, c0 + off + h))
    return pl.pallas_call(
        _dilated_kernel,
        out_shape=jax.ShapeDtypeStruct((m, b_width), BF16),
        grid=(batch, n_heads),
        in_specs=[spec(0), spec(n_heads), spec(2 * n_heads)],
        out_specs=pl.BlockSpec((seq, HEAD_DIM), lambda b, h: (b, h)),
        scratch_shapes=[pltpu.VMEM((3, seq, HEAD_DIM), F32), pltpu.VMEM((3, seq, HEAD_DIM), F32),
                        pltpu.VMEM((3, seq, HEAD_DIM), BF16), pltpu.VMEM((2, seq, HEAD_DIM), F32),
                        pltpu.VMEM((2, seq, HEAD_DIM), F32), pltpu.VMEM((2, seq, HEAD_DIM), F32)],
        compiler_params=_params(2, vmem),
        name="dilated_attn",
    )(proj, proj, proj)


WIN_TILES_PER_STEP = 2


def _window_kernel(sink_ref, q_ref, k_ref, v_ref, o_ref, *, group, layer):
    seq = k_ref.shape[0]
    hk = pl.program_id(1)
    tq, kw, n_tiles = _band_geometry(seq, WIN_HALF)
    per_step = math.gcd(n_tiles, WIN_TILES_PER_STEP)

    def tiles_body(step, carry):
        dests, tiles, sinks = [], [], []
        for tt in range(per_step):
            q0 = pl.multiple_of((step * per_step + tt) * tq, tq)
            k0 = _band_key_start(q0, seq, kw, WIN_HALF)
            qrows, krows = pl.ds(q0, tq), pl.ds(k0, kw)
            k = k_ref[krows, :]
            v = v_ref[krows, :]
            valid = _band_mask(tq, kw, k0 - q0, WIN_HALF)
            for g in range(group):
                cols = slice(g * HEAD_DIM, (g + 1) * HEAD_DIM)
                dests.append((qrows, cols))
                tiles.append((q_ref[qrows, cols], k, v, valid))
                sinks.append(sink_ref[layer, hk * group + g])
        for (qrows, cols), (o, _) in zip(dests, _band_tiles(tiles, sinks)):
            o_ref[qrows, cols] = o.astype(o_ref.dtype)
        return carry

    lax.fori_loop(0, n_tiles // per_step, tiles_body, 0)


def _window_attn(proj, sinks, layer, batch, seq, n_q_heads, n_kv_heads):
    m = proj.shape[0]
    group = n_q_heads // n_kv_heads
    gw = group * HEAD_DIM
    vmem = 2 * (2 * seq * gw * 2 + 2 * seq * HEAD_DIM * 2) + 16 * 1024 * 1024
    return pl.pallas_call(
        functools.partial(_window_kernel, group=group, layer=layer),
        out_shape=jax.ShapeDtypeStruct((m, n_q_heads * HEAD_DIM), BF16),
        grid_spec=pltpu.PrefetchScalarGridSpec(
            num_scalar_prefetch=1,
            grid=(batch, n_kv_heads),
            in_specs=[pl.BlockSpec((seq, gw), lambda b, h, s: (b, h)),
                      pl.BlockSpec((seq, HEAD_DIM), lambda b, h, s: (b, n_q_heads + h)),
                      pl.BlockSpec((seq, HEAD_DIM), lambda b, h, s: (b, n_q_heads + n_kv_heads + h))],
            out_specs=pl.BlockSpec((seq, gw), lambda b, h, s: (b, h))),
        compiler_params=_params(2, vmem),
        name="window_attn",
    )(sinks, proj, proj, proj)


def _rope_tables(seq):
    inv_freq = ROPE_THETA ** (-jnp.arange(0, HEAD_DIM, 2, dtype=F32) / HEAD_DIM)
    ang = jnp.arange(seq, dtype=F32)[:, None] * inv_freq[None, :]
    ang = jnp.concatenate([ang, ang], axis=-1)
    sign = jnp.where(jnp.arange(HEAD_DIM) < HEAD_DIM // 2, -1.0, 1.0).astype(F32)
    return jnp.cos(ang), jnp.sin(ang) * sign


def _section_gains(sections):
    cols, flags = [], []
    for width, g in sections:
        reps = width // HEAD_DIM
        if g is None:
            cols.append(jnp.ones((width,), F32))
            flags += [0] * reps
        else:
            cols.append(jnp.tile(g.astype(F32), reps))
            flags += [1] * reps
    return jnp.concatenate(cols).reshape(1, -1), flags


def _tile_flags(flags, tn):
    per = tn // HEAD_DIM
    tiles = [flags[i:i + per] for i in range(0, len(flags), per)]
    assert all(len(set(t)) == 1 for t in tiles), "a column tile mixes normed and plain heads"
    return jnp.asarray([t[0] for t in tiles], jnp.int32)


def kernel(x, mix_norm, ffn_norm, w_gate, w_up, w_down, hy_w_in, hy_w_out, diff_q_norm, diff_k_norm,
           diff_lambda, diff_subln, dil_q_norm, dil_k_norm, win_w_in, win_w_out, win_q_norm, win_k_norm,
           win_sink):
    batch, seq, d_model = x.shape
    depth = mix_norm.shape[0]
    a_width = d_model // 2
    b_width = d_model - a_width
    n_q_heads = d_model // HEAD_DIM
    n_kv_heads = (win_w_in.shape[-1] // HEAD_DIM - n_q_heads) // 2
    scale = HEAD_DIM ** -0.5

    cos, sin = _rope_tables(seq)
    xf = x.reshape(batch * seq, d_model)
    mix_gain = mix_norm.astype(F32)[:, None, :]
    ffn_gain = ffn_norm.astype(F32)[:, None, :]
    w_gate, w_up, w_down = (w.astype(BF16) for w in (w_gate, w_up, w_down))
    hy_w_in, hy_w_out, win_w_in, win_w_out = (w.astype(BF16) for w in (hy_w_in, hy_w_out, win_w_in, win_w_out))
    diff_lambda = diff_lambda.astype(F32)
    diff_subln = diff_subln.astype(F32)[:, None, :]
    win_sink = win_sink.astype(F32)

    for layer in range(depth):
        h = _rmsnorm(xf, mix_gain, layer)
        if layer % 2 == 0:
            e = layer // 2
            gains, flags = _section_gains([
                (a_width, diff_q_norm[e] * scale), (a_width, diff_k_norm[e]), (a_width, None),
                (b_width, dil_q_norm[e] * scale), (b_width, dil_k_norm[e]), (b_width, None)])
            tn = _largest_tile(math.gcd(a_width, b_width), 1024)
            proj = _inproj(h, hy_w_in, e, gains, _tile_flags(flags, tn), cos, sin, tn)
            lam_init = 0.8 - 0.6 * math.exp(-0.3 * layer)
            ao = _diff_attn(proj, diff_lambda, diff_subln, e, batch, seq, a_width, lam_init)
            bo = _dilated_attn(proj, batch, seq, 3 * a_width, b_width)
            xf = _mm2_res(ao, bo, hy_w_out, e, xf)
        else:
            o = layer // 2
            qd, kd = n_q_heads * HEAD_DIM, n_kv_heads * HEAD_DIM
            gains, flags = _section_gains([
                (qd, win_q_norm[o] * scale), (kd, win_k_norm[o]), (kd, None)])
            tn = _largest_tile(math.gcd(qd, kd), 1024)
            proj = _inproj(h, win_w_in, o, gains, _tile_flags(flags, tn), cos, sin, tn)
            att = _window_attn(proj, win_sink, o, batch, seq, n_q_heads, n_kv_heads)
            xf = _mm_res(att, win_w_out, o, xf)
        h = _rmsnorm(xf, ffn_gain, layer)
        hidden = _gateup(h, w_gate, w_up, layer)
        xf = _down(hidden, w_down, layer, xf)
    return xf.reshape(batch, seq, d_model)
```

```python
import functools
import math

import jax
import jax.numpy as jnp
from jax import lax
from jax.experimental import pallas as pl
from jax.experimental.pallas import tpu as pltpu

HEAD_DIM = 128
DIL_PATTERNS = ((128, 1), (512, 4), (2048, 16))
WIN_HALF = 128
ROPE_THETA = 10000.0
EPS = 1e-6
NEG_INF = -1e30
LOG2_E = 1.4426950408889634

V7X_LANES = 128
V7X_MXU_COLS = 256
V7X_VMEM_LIMIT_CAP = 56 * 1024 * 1024

F32 = jnp.float32
BF16 = jnp.bfloat16
NT_DIMS = (((1,), (1,)), ((), ()))


def _params(n_grid_dims, vmem_bytes):
    return pltpu.CompilerParams(
        dimension_semantics=("arbitrary",) * n_grid_dims,
        vmem_limit_bytes=int(min(vmem_bytes, V7X_VMEM_LIMIT_CAP)))


def _largest_tile(n, cap, quantum=V7X_LANES):
    best = None
    t = quantum
    while t <= min(n, cap):
        if n % t == 0:
            best = t
        t += quantum
    assert best is not None, (n, cap)
    return best


def _lane_partial_sumsq(x):
    sq = x * x
    return functools.reduce(lambda a, b: a + b,
                            [sq[:, c:c + V7X_LANES] for c in range(0, x.shape[1], V7X_LANES)])


def _inv_rms(ssq_ref, d_model):
    return lax.rsqrt(jnp.sum(ssq_ref[...], axis=-1, keepdims=True) * (1.0 / d_model) + EPS)


def _prenorm_kernel(x_ref, g_ref, xg_ref, ssq_ref):
    x = x_ref[...]
    xg_ref[...] = (x * g_ref[...]).astype(xg_ref.dtype)
    ssq_ref[...] = _lane_partial_sumsq(x)


def _prenorm(x, gains, layer):
    m, d = x.shape
    tm = _largest_tile(m, 256, 8)
    return pl.pallas_call(
        _prenorm_kernel,
        out_shape=[jax.ShapeDtypeStruct((m, d), BF16), jax.ShapeDtypeStruct((m, V7X_LANES), F32)],
        grid=(m // tm,),
        in_specs=[pl.BlockSpec((tm, d), lambda i: (i, 0)),
                  pl.BlockSpec((None, 1, d), lambda i: (layer, 0, 0))],
        out_specs=[pl.BlockSpec((tm, d), lambda i: (i, 0)),
                   pl.BlockSpec((tm, V7X_LANES), lambda i: (i, 0))],
        compiler_params=_params(1, 2 * tm * d * (4 + 2) + 4 * tm * d * 4),
        name="prenorm",
    )(x, gains)


def _emit_residual(res, rest, with_next):
    if not with_next:
        (o_ref,) = rest
        o_ref[...] = res
        return
    g_ref, o_ref, xg_ref, ssq_ref = rest
    o_ref[...] = res
    xg_ref[...] = (res * g_ref[...]).astype(xg_ref.dtype)
    part = _lane_partial_sumsq(res)
    j = pl.program_id(1)

    @pl.when(j == 0)
    def _():
        ssq_ref[...] = part

    @pl.when(j != 0)
    def _():
        ssq_ref[...] += part


def _residual_specs(m, n, tm, tn, next_norm):
    out_shape = [jax.ShapeDtypeStruct((m, n), F32)]
    out_specs = [pl.BlockSpec((tm, tn), lambda i, j: (i, j))]
    if next_norm is None:
        return [], [], out_shape[0], out_specs[0]
    gains, layer = next_norm
    out_shape += [jax.ShapeDtypeStruct((m, n), BF16), jax.ShapeDtypeStruct((m, V7X_LANES), F32)]
    out_specs += [pl.BlockSpec((tm, tn), lambda i, j: (i, j)),
                  pl.BlockSpec((tm, V7X_LANES), lambda i, j: (i, 0))]
    return [pl.BlockSpec((None, 1, tn), lambda i, j: (layer, 0, j))], [gains], out_shape, out_specs


def _norm_rope(x, g, cos, sin):
    ms = jnp.mean(x * x, axis=-1, keepdims=True)
    y = x * lax.rsqrt(ms + EPS) * g
    return y * cos + pltpu.roll(y, HEAD_DIM // 2, axis=1) * sin


def _inproj_kernel(flag_ref, a_ref, ssq_ref, b_ref, g_ref, cos_ref, sin_ref, o_ref):
    j = pl.program_id(1)
    acc = jnp.dot(a_ref[...], b_ref[...], preferred_element_type=F32) * _inv_rms(ssq_ref, a_ref.shape[1])

    @pl.when(flag_ref[j] == 1)
    def _():
        cos = cos_ref[...]
        sin = sin_ref[...]
        for c in range(acc.shape[1] // HEAD_DIM):
            sl = slice(c * HEAD_DIM, (c + 1) * HEAD_DIM)
            o_ref[:, sl] = _norm_rope(acc[:, sl], g_ref[:, sl], cos, sin).astype(o_ref.dtype)

    @pl.when(flag_ref[j] == 0)
    def _():
        o_ref[...] = acc.astype(o_ref.dtype)


def _inproj(xg, ssq, w, layer, gain_cols, flags, cos, sin, tn):
    m, k = xg.shape
    n = w.shape[2]
    s = cos.shape[0]
    tm = _largest_tile(s, 1024, 8)
    n_pos_blocks = s // tm
    vmem = 2 * (tm * k * 2 + k * tn * 2 + tm * tn * 2 + 2 * tm * HEAD_DIM * 4) + 3 * tm * tn * 4
    return pl.pallas_call(
        _inproj_kernel,
        out_shape=jax.ShapeDtypeStruct((m, n), BF16),
        grid_spec=pltpu.PrefetchScalarGridSpec(
            num_scalar_prefetch=1,
            grid=(m // tm, n // tn),
            in_specs=[pl.BlockSpec((tm, k), lambda i, j, f: (i, 0)),
                      pl.BlockSpec((tm, V7X_LANES), lambda i, j, f: (i, 0)),
                      pl.BlockSpec((None, k, tn), lambda i, j, f: (layer, 0, j)),
                      pl.BlockSpec((1, tn), lambda i, j, f: (0, j)),
                      pl.BlockSpec((tm, HEAD_DIM), lambda i, j, f: (i % n_pos_blocks, 0)),
                      pl.BlockSpec((tm, HEAD_DIM), lambda i, j, f: (i % n_pos_blocks, 0))],
            out_specs=pl.BlockSpec((tm, tn), lambda i, j, f: (i, j))),
        compiler_params=_params(2, vmem),
        name="inproj",
    )(flags, xg, ssq, w, gain_cols, cos, sin)


def _mm_res_kernel(a_ref, b_ref, r_ref, *rest, with_next):
    _emit_residual(r_ref[...] + jnp.dot(a_ref[...], b_ref[...], preferred_element_type=F32), rest, with_next)


def _mm_res(a, w, layer, r, next_norm, tm_cap, name):
    m, k = a.shape
    n = w.shape[2]
    tm = _largest_tile(m, tm_cap, 8)
    tn = _largest_tile(n, 512)
    extra_specs, extra_ops, out_shape, out_specs = _residual_specs(m, n, tm, tn, next_norm)
    vmem = 2 * (tm * k * 2 + k * tn * 2 + 2 * tm * tn * 4 + tm * tn * 2 + tm * V7X_LANES * 4) + 4 * tm * tn * 4
    return pl.pallas_call(
        functools.partial(_mm_res_kernel, with_next=next_norm is not None),
        out_shape=out_shape,
        grid=(m // tm, n // tn),
        in_specs=[pl.BlockSpec((tm, k), lambda i, j: (i, 0)),
                  pl.BlockSpec((None, k, tn), lambda i, j: (layer, 0, j)),
                  pl.BlockSpec((tm, tn), lambda i, j: (i, j))] + extra_specs,
        out_specs=out_specs,
        compiler_params=_params(2, vmem),
        name=name,
    )(a, w, r, *extra_ops)


def _mm2_res_kernel(a1_ref, b1_ref, a2_ref, b2_ref, r_ref, *rest, with_next):
    acc = jnp.dot(a1_ref[...], b1_ref[...], preferred_element_type=F32)
    acc += jnp.dot(a2_ref[...], b2_ref[...], preferred_element_type=F32)
    _emit_residual(r_ref[...] + acc, rest, with_next)


def _mm2_res(a1, a2, w, layer, r, next_norm):
    m, k1 = a1.shape
    k2 = a2.shape[1]
    assert k1 == k2, "the two head groups index w[layer] as row blocks 0 and 1"
    n = w.shape[2]
    tm = _largest_tile(m, 1024, 8)
    tn = _largest_tile(n, 512)
    extra_specs, extra_ops, out_shape, out_specs = _residual_specs(m, n, tm, tn, next_norm)
    vmem = (2 * (tm * (k1 + k2) * 2 + (k1 + k2) * tn * 2 + 2 * tm * tn * 4 + tm * tn * 2 + tm * V7X_LANES * 4)
            + 4 * tm * tn * 4)
    return pl.pallas_call(
        functools.partial(_mm2_res_kernel, with_next=next_norm is not None),
        out_shape=out_shape,
        grid=(m // tm, n // tn),
        in_specs=[pl.BlockSpec((tm, k1), lambda i, j: (i, 0)),
                  pl.BlockSpec((None, k1, tn), lambda i, j: (layer, 0, j)),
                  pl.BlockSpec((tm, k2), lambda i, j: (i, 0)),
                  pl.BlockSpec((None, k2, tn), lambda i, j: (layer, 1, j)),
                  pl.BlockSpec((tm, tn), lambda i, j: (i, j))] + extra_specs,
        out_specs=out_specs,
        compiler_params=_params(2, vmem),
        name="outproj2",
    )(a1, w, a2, w, r, *extra_ops)


def _gateup_kernel(a_ref, ssq_ref, wg_ref, wu_ref, o_ref):
    a = a_ref[...]
    inv = _inv_rms(ssq_ref, a_ref.shape[1])
    g = jnp.dot(a, wg_ref[...], preferred_element_type=F32)
    u = jnp.dot(a, wu_ref[...], preferred_element_type=F32)
    e = jnp.exp2(g * (inv * -LOG2_E))
    o_ref[...] = ((g * u) * ((inv * inv) / (1.0 + e))).astype(o_ref.dtype)


def _gateup(xg, ssq, wg, wu, layer):
    m, k = xg.shape
    n = wg.shape[2]
    tm = _largest_tile(m, 2048, 8)
    tn = _largest_tile(n, 512, V7X_MXU_COLS)
    vmem = 2 * (tm * k * 2 + 2 * k * tn * 2 + tm * tn * 2) + 6 * tm * tn * 4
    return pl.pallas_call(
        _gateup_kernel,
        out_shape=jax.ShapeDtypeStruct((m, n), BF16),
        grid=(m // tm, n // tn),
        in_specs=[pl.BlockSpec((tm, k), lambda i, j: (i, 0)),
                  pl.BlockSpec((tm, V7X_LANES), lambda i, j: (i, 0)),
                  pl.BlockSpec((None, k, tn), lambda i, j: (layer, 0, j)),
                  pl.BlockSpec((None, k, tn), lambda i, j: (layer, 0, j))],
        out_specs=pl.BlockSpec((tm, tn), lambda i, j: (i, j)),
        compiler_params=_params(2, vmem),
        name="gateup",
    )(xg, ssq, wg, wu)


DIFF_ROW_CHUNK = 256


def _diff_attn_kernel(q_ref, k_ref, v_ref, lam_ref, subln_ref, o_ref, *, lam_init):
    q = q_ref[...]
    k = k_ref[...]
    v = v_ref[...]
    lf = lam_ref[...]
    lam = (jnp.exp(jnp.sum(lf[0:1] * lf[1:2], axis=-1, keepdims=True))
           - jnp.exp(jnp.sum(lf[2:3] * lf[3:4], axis=-1, keepdims=True)) + lam_init)

    halves = (slice(0, HEAD_DIM), slice(HEAD_DIM, 2 * HEAD_DIM))
    tq = q.shape[0]
    chunk = math.gcd(tq, DIFF_ROW_CHUNK)
    rows = [slice(r0, r0 + chunk) for r0 in range(0, tq, chunk)]

    def scores(rs):
        return [lax.dot_general(q[rs, c], k[:, c], NT_DIMS, preferred_element_type=F32) for c in halves]

    def finish(ss):
        probs = []
        for s in ss:
            p = jnp.exp2(s - jnp.max(s, axis=-1, keepdims=True))
            probs.append((p.astype(BF16), jnp.sum(p, axis=-1, keepdims=True)))
        o1, o2 = [jnp.dot(p, v, preferred_element_type=F32) / l for p, l in probs]
        return o1 - lam * o2

    pending = scores(rows[0])
    outs = []
    for nxt in rows[1:]:
        ahead = scores(nxt)
        outs.append(finish(pending))
        pending = ahead
    outs.append(finish(pending))
    o = jnp.concatenate(outs, axis=0)
    ms = jnp.mean(o * o, axis=-1, keepdims=True)
    y = o * lax.rsqrt(ms + EPS) * subln_ref[...]
    o_ref[...] = (y * (1.0 - lam_init)).astype(o_ref.dtype)


def _diff_attn(proj, lambdas, subln, layer, batch, seq, a_width, lam_init):
    m = proj.shape[0]
    hd2 = 2 * HEAD_DIM
    n_heads = a_width // hd2
    tq = _largest_tile(seq, 1024, 8)
    nq = seq // tq
    vmem = 2 * (2 * tq * hd2 * 2 + 2 * seq * hd2 * 2) + 6 * tq * seq * 4
    return pl.pallas_call(
        functools.partial(_diff_attn_kernel, lam_init=lam_init),
        out_shape=jax.ShapeDtypeStruct((m, a_width), BF16),
        grid=(batch, n_heads, nq),
        in_specs=[pl.BlockSpec((tq, hd2), lambda b, h, t: (b * nq + t, h)),
                  pl.BlockSpec((seq, hd2), lambda b, h, t: (b, n_heads + h)),
                  pl.BlockSpec((seq, hd2), lambda b, h, t: (b, 2 * n_heads + h)),
                  pl.BlockSpec((None, 4, HEAD_DIM), lambda b, h, t: (layer, 0, 0)),
                  pl.BlockSpec((None, 1, hd2), lambda b, h, t: (layer, 0, 0))],
        out_specs=pl.BlockSpec((tq, hd2), lambda b, h, t: (b * nq + t, h)),
        compiler_params=_params(3, vmem),
        name="diff_attn",
    )(proj, proj, proj, lambdas, subln)


BAND_TQ = 128


def _band_geometry(n, half_width):
    tq = min(BAND_TQ, n)
    kw = min(n, tq + 2 * half_width)
    return tq, kw, n // tq


def _band_key_start(q0, n, kw, half_width):
    return pl.multiple_of(jnp.clip(q0 - half_width, 0, n - kw), half_width)


def _band_mask(tq, kw, k0_minus_q0, half_width):
    rel = (lax.broadcasted_iota(jnp.int32, (tq, kw), 1) + k0_minus_q0
           - lax.broadcasted_iota(jnp.int32, (tq, kw), 0))
    return jnp.abs(rel) <= half_width


def _band_tiles(tiles, sinks=None):
    sinks = [None] * len(tiles) if sinks is None else sinks
    scores = [lax.dot_general(q, k, NT_DIMS, preferred_element_type=F32) for (q, k, _, _) in tiles]
    probs = []
    for s, (_, _, _, valid), sk in zip(scores, tiles, sinks):
        s = jnp.where(valid, s, NEG_INF)
        m = jnp.max(s, axis=-1, keepdims=True)
        if sk is not None:
            m = jnp.maximum(m, sk)
        p = jnp.exp2(s - m)
        l = jnp.sum(p, axis=-1, keepdims=True)
        if sk is not None:
            l = l + jnp.exp2(sk - m)
        probs.append((p.astype(BF16), m, l))
    return [(jnp.dot(p, v, preferred_element_type=F32) / l, m + jnp.log2(l))
            for (p, m, l), (_, _, v, _) in zip(probs, tiles)]


def _fold(dst_ref, src_ref, factor, n_blocks, cast=None):
    n_tensors, rows = src_ref.shape[0], src_ref.shape[1]
    src_block = rows // n_blocks
    dst_block = src_block // factor
    for i in range(n_tensors):
        for j in range(n_blocks):
            for c in range(factor):
                v = src_ref[i, pl.ds(j * src_block + c, dst_block, stride=factor), :]
                dst_ref[i, (j * factor + c) * dst_block:(j * factor + c + 1) * dst_block, :] = (
                    v if cast is None else v.astype(cast))


def _unfold(dst_ref, src_ref, factor, n_blocks):
    n_tensors, rows = src_ref.shape[0], src_ref.shape[1]
    dst_block = rows // n_blocks
    src_block = dst_block // factor
    for i in range(n_tensors):
        for j in range(n_blocks):
            for c in range(factor):
                dst_ref[i, pl.ds(j * dst_block + c, src_block, stride=factor), :] = (
                    src_ref[i, (j * factor + c) * src_block:(j * factor + c + 1) * src_block, :])


def _dilated_kernel(q_ref, k_ref, v_ref, o_ref, nat_ref, x4_ref, x16_ref, r16_ref, u4_ref, r4_ref):
    seq = q_ref.shape[0]
    (w1, d1), (w4, d4), (w16, d16) = DIL_PATTERNS
    assert (d1, d4, d16) == (1, 4, 16) and w1 // (2 * d1) == w4 // (2 * d4) == w16 // (2 * d16)
    hw = w1 // 2

    for i, ref in enumerate((q_ref, k_ref, v_ref)):
        nat_ref[i] = ref[...].astype(F32)
    _fold(x4_ref, nat_ref, 4, 1)
    _fold(x16_ref, x4_ref, 4, 4, cast=BF16)

    sub16 = seq // 16
    tq, kw, n_tiles = _band_geometry(sub16, hw)
    assert n_tiles == 1 and kw == sub16
    valid16 = _band_mask(tq, kw, 0, hw)
    blocks = [slice(r * sub16, (r + 1) * sub16) for r in range(16)]
    tiles = [(x16_ref[0, rows, :], x16_ref[1, rows, :], x16_ref[2, rows, :], valid16) for rows in blocks]
    for rows, (o, lse) in zip(blocks, _band_tiles(tiles)):
        r16_ref[0, rows, :] = o
        r16_ref[1, rows, :] = jnp.broadcast_to(lse, (sub16, HEAD_DIM))
    _unfold(u4_ref, r16_ref, 4, 4)

    sub4 = seq // 4
    tq, kw, n_tiles = _band_geometry(sub4, hw)
    blocks, tiles = [], []
    for c in range(4):
        for t in range(n_tiles):
            q0 = t * tq
            k0 = min(max(q0 - hw, 0), sub4 - kw)
            qrows = slice(c * sub4 + q0, c * sub4 + q0 + tq)
            krows = slice(c * sub4 + k0, c * sub4 + k0 + kw)
            blocks.append(qrows)
            tiles.append((x4_ref[0, qrows, :].astype(BF16), x4_ref[1, krows, :].astype(BF16),
                          x4_ref[2, krows, :].astype(BF16), _band_mask(tq, kw, k0 - q0, hw)))
    for qrows, (o, lse) in zip(blocks, _band_tiles(tiles)):
        r4_ref[0, qrows, :] = o
        r4_ref[1, qrows, :] = jnp.broadcast_to(lse, (tq, HEAD_DIM))
    _unfold(r16_ref, u4_ref, 4, 1)
    _unfold(nat_ref, r4_ref, 4, 1)

    tq, kw, n_tiles = _band_geometry(seq, hw)
    blocks, tiles = [], []
    for t in range(n_tiles):
        q0 = t * tq
        k0 = min(max(q0 - hw, 0), seq - kw)
        qrows, krows = slice(q0, q0 + tq), slice(k0, k0 + kw)
        blocks.append(qrows)
        tiles.append((q_ref[qrows, :], k_ref[krows, :], v_ref[krows, :], _band_mask(tq, kw, k0 - q0, hw)))
    for qrows, (o, lse) in zip(blocks, _band_tiles(tiles)):
        lses = [lse, nat_ref[1, qrows, :], r16_ref[1, qrows, :]]
        outs = [o, nat_ref[0, qrows, :], r16_ref[0, qrows, :]]
        top = functools.reduce(jnp.maximum, lses)
        ws = [jnp.exp2(x - top) for x in lses]
        num = functools.reduce(lambda a, b: a + b, [w * x for w, x in zip(ws, outs)])
        den = functools.reduce(lambda a, b: a + b, ws)
        o_ref[qrows, :] = (num / den).astype(o_ref.dtype)


def _dilated_attn(proj, batch, seq, col0, b_width):
    m = proj.shape[0]
    n_heads = b_width // HEAD_DIM
    c0 = col0 // HEAD_DIM
    plane = seq * HEAD_DIM
    scratch_bytes = (3 + 3 + 2 + 2 + 2) * plane * 4 + 3 * plane * 2
    vmem = 2 * 4 * plane * 2 + scratch_bytes + 12 * 1024 * 1024
    spec = lambda off: pl.BlockSpec((seq, HEAD_DIM), lambda b, h: (b, c0 + off + h))
    return pl.pallas_call(
        _dilated_kernel,
        out_shape=jax.ShapeDtypeStruct((m, b_width), BF16),
        grid=(batch, n_heads),
        in_specs=[spec(0), spec(n_heads), spec(2 * n_heads)],
        out_specs=pl.BlockSpec((seq, HEAD_DIM), lambda b, h: (b, h)),
        scratch_shapes=[pltpu.VMEM((3, seq, HEAD_DIM), F32), pltpu.VMEM((3, seq, HEAD_DIM), F32),
                        pltpu.VMEM((3, seq, HEAD_DIM), BF16), pltpu.VMEM((2, seq, HEAD_DIM), F32),
                        pltpu.VMEM((2, seq, HEAD_DIM), F32), pltpu.VMEM((2, seq, HEAD_DIM), F32)],
        compiler_params=_params(2, vmem),
        name="dilated_attn",
    )(proj, proj, proj)


WIN_TILES_PER_STEP = 4


def _window_kernel(sink_ref, q_ref, k_ref, v_ref, o_ref, *, group, layer):
    seq = k_ref.shape[0]
    hk = pl.program_id(1)
    tq, kw, n_tiles = _band_geometry(seq, WIN_HALF)
    per_step = math.gcd(n_tiles, WIN_TILES_PER_STEP)

    def tiles_body(step, carry):
        dests, tiles, sinks = [], [], []
        for tt in range(per_step):
            q0 = pl.multiple_of((step * per_step + tt) * tq, tq)
            k0 = _band_key_start(q0, seq, kw, WIN_HALF)
            qrows, krows = pl.ds(q0, tq), pl.ds(k0, kw)
            k = k_ref[krows, :]
            v = v_ref[krows, :]
            valid = _band_mask(tq, kw, k0 - q0, WIN_HALF)
            for g in range(group):
                cols = slice(g * HEAD_DIM, (g + 1) * HEAD_DIM)
                dests.append((qrows, cols))
                tiles.append((q_ref[qrows, cols], k, v, valid))
                sinks.append(sink_ref[layer, hk * group + g] * LOG2_E)
        for (qrows, cols), (o, _) in zip(dests, _band_tiles(tiles, sinks)):
            o_ref[qrows, cols] = o.astype(o_ref.dtype)
        return carry

    lax.fori_loop(0, n_tiles // per_step, tiles_body, 0)


def _window_attn(proj, sinks, layer, batch, seq, n_q_heads, n_kv_heads):
    m = proj.shape[0]
    group = n_q_heads // n_kv_heads
    gw = group * HEAD_DIM
    vmem = 2 * (2 * seq * gw * 2 + 2 * seq * HEAD_DIM * 2) + 16 * 1024 * 1024
    return pl.pallas_call(
        functools.partial(_window_kernel, group=group, layer=layer),
        out_shape=jax.ShapeDtypeStruct((m, n_q_heads * HEAD_DIM), BF16),
        grid_spec=pltpu.PrefetchScalarGridSpec(
            num_scalar_prefetch=1,
            grid=(batch, n_kv_heads),
            in_specs=[pl.BlockSpec((seq, gw), lambda b, h, s: (b, h)),
                      pl.BlockSpec((seq, HEAD_DIM), lambda b, h, s: (b, n_q_heads + h)),
                      pl.BlockSpec((seq, HEAD_DIM), lambda b, h, s: (b, n_q_heads + n_kv_heads + h))],
            out_specs=pl.BlockSpec((seq, gw), lambda b, h, s: (b, h))),
        compiler_params=_params(2, vmem),
        name="window_attn",
    )(sinks, proj, proj, proj)


def _rope_tables(seq):
    inv_freq = ROPE_THETA ** (-jnp.arange(0, HEAD_DIM, 2, dtype=F32) / HEAD_DIM)
    ang = jnp.arange(seq, dtype=F32)[:, None] * inv_freq[None, :]
    ang = jnp.concatenate([ang, ang], axis=-1)
    sign = jnp.where(jnp.arange(HEAD_DIM) < HEAD_DIM // 2, -1.0, 1.0).astype(F32)
    return jnp.cos(ang), jnp.sin(ang) * sign


def _section_gains(sections):
    cols, flags = [], []
    for width, g in sections:
        reps = width // HEAD_DIM
        if g is None:
            cols.append(jnp.ones((width,), F32))
            flags += [0] * reps
        else:
            cols.append(jnp.tile(g.astype(F32), reps))
            flags += [1] * reps
    return jnp.concatenate(cols).reshape(1, -1), flags


def _tile_flags(flags, tn):
    per = tn // HEAD_DIM
    tiles = [flags[i:i + per] for i in range(0, len(flags), per)]
    assert all(len(set(t)) == 1 for t in tiles), "a column tile mixes normed and plain heads"
    return jnp.asarray([t[0] for t in tiles], jnp.int32)


def kernel(x, mix_norm, ffn_norm, w_gate, w_up, w_down, hy_w_in, hy_w_out, diff_q_norm, diff_k_norm,
           diff_lambda, diff_subln, dil_q_norm, dil_k_norm, win_w_in, win_w_out, win_q_norm, win_k_norm,
           win_sink):
    batch, seq, d_model = x.shape
    depth = mix_norm.shape[0]
    a_width = d_model // 2
    b_width = d_model - a_width
    n_q_heads = d_model // HEAD_DIM
    n_kv_heads = (win_w_in.shape[-1] // HEAD_DIM - n_q_heads) // 2
    scale = HEAD_DIM ** -0.5 * LOG2_E

    cos, sin = _rope_tables(seq)
    xf = x.reshape(batch * seq, d_model)
    mix_gain = mix_norm.astype(F32)[:, None, :]
    ffn_gain = ffn_norm.astype(F32)[:, None, :]
    w_gate, w_up, w_down = (w.astype(BF16) for w in (w_gate, w_up, w_down))
    hy_w_in, hy_w_out, win_w_in, win_w_out = (w.astype(BF16) for w in (hy_w_in, hy_w_out, win_w_in, win_w_out))
    diff_lambda = diff_lambda.astype(F32)
    diff_subln = diff_subln.astype(F32)[:, None, :]
    win_sink = win_sink.astype(F32)

    xg, ssq = _prenorm(xf, mix_gain, 0)
    for layer in range(depth):
        ffn_norm_next = (ffn_gain, layer)
        if layer % 2 == 0:
            e = layer // 2
            gains, flags = _section_gains([
                (a_width, diff_q_norm[e] * scale), (a_width, diff_k_norm[e]), (a_width, None),
                (b_width, dil_q_norm[e] * scale), (b_width, dil_k_norm[e]), (b_width, None)])
            tn = _largest_tile(math.gcd(a_width, b_width), 1024)
            proj = _inproj(xg, ssq, hy_w_in, e, gains, _tile_flags(flags, tn), cos, sin, tn)
            lam_init = 0.8 - 0.6 * math.exp(-0.3 * layer)
            ao = _diff_attn(proj, diff_lambda, diff_subln, e, batch, seq, a_width, lam_init)
            bo = _dilated_attn(proj, batch, seq, 3 * a_width, b_width)
            xf, xg, ssq = _mm2_res(ao, bo, hy_w_out, e, xf, ffn_norm_next)
        else:
            o = layer // 2
            qd, kd = n_q_heads * HEAD_DIM, n_kv_heads * HEAD_DIM
            gains, flags = _section_gains([
                (qd, win_q_norm[o] * scale), (kd, win_k_norm[o]), (kd, None)])
            tn = _largest_tile(math.gcd(qd, kd), 1024)
            proj = _inproj(xg, ssq, win_w_in, o, gains, _tile_flags(flags, tn), cos, sin, tn)
            att = _window_attn(proj, win_sink, o, batch, seq, n_q_heads, n_kv_heads)
            xf, xg, ssq = _mm_res(att, win_w_out, o, xf, ffn_norm_next, 1024, "outproj")
        hidden = _gateup(xg, ssq, w_gate, w_up, layer)
        if layer + 1 < depth:
            xf, xg, ssq = _mm_res(hidden, w_down, layer, xf, (mix_gain, layer + 1), 512, "down")
        else:
            xf = _mm_res(hidden, w_down, layer, xf, None, 512, "down")
    return xf.reshape(batch, seq, d_model)
```

```python
import functools
import math

import jax
import jax.numpy as jnp
from jax import lax
from jax.experimental import pallas as pl
from jax.experimental.pallas import tpu as pltpu

HEAD_DIM = 128
DIL_PATTERNS = ((128, 1), (512, 4), (2048, 16))
WIN_HALF = 128
ROPE_THETA = 10000.0
EPS = 1e-6
NEG_INF = -1e30
LOG2_E = 1.4426950408889634

V7X_LANES = 128
V7X_MXU_COLS = 256
V7X_VMEM_LIMIT_CAP = 56 * 1024 * 1024

F32 = jnp.float32
BF16 = jnp.bfloat16
NT_DIMS = (((1,), (1,)), ((), ()))


def _params(n_grid_dims, vmem_bytes):
    return pltpu.CompilerParams(
        dimension_semantics=("arbitrary",) * n_grid_dims,
        vmem_limit_bytes=int(min(vmem_bytes, V7X_VMEM_LIMIT_CAP)))


def _largest_tile(n, cap, quantum=V7X_LANES):
    best = None
    t = quantum
    while t <= min(n, cap):
        if n % t == 0:
            best = t
        t += quantum
    assert best is not None, (n, cap)
    return best


def _lane_partial_sumsq(x):
    sq = x * x
    return functools.reduce(lambda a, b: a + b,
                            [sq[:, c:c + V7X_LANES] for c in range(0, x.shape[1], V7X_LANES)])


def _inv_rms(ssq_ref, rows, d_model):
    return lax.rsqrt(jnp.sum(ssq_ref[rows, :], axis=-1, keepdims=True) * (1.0 / d_model) + EPS)


ROW_PARTS = 4


def _row_parts(tm):
    part = tm // ROW_PARTS
    return [slice(r0, r0 + part) for r0 in range(0, tm, part)]


def _prenorm_kernel(x_ref, g_ref, xg_ref, ssq_ref):
    x = x_ref[...]
    xg_ref[...] = (x * g_ref[...]).astype(xg_ref.dtype)
    ssq_ref[...] = _lane_partial_sumsq(x)


def _prenorm(x, gains, layer):
    m, d = x.shape
    tm = _largest_tile(m, 256, 8)
    return pl.pallas_call(
        _prenorm_kernel,
        out_shape=[jax.ShapeDtypeStruct((m, d), BF16), jax.ShapeDtypeStruct((m, V7X_LANES), F32)],
        grid=(m // tm,),
        in_specs=[pl.BlockSpec((tm, d), lambda i: (i, 0)),
                  pl.BlockSpec((None, 1, d), lambda i: (layer, 0, 0))],
        out_specs=[pl.BlockSpec((tm, d), lambda i: (i, 0)),
                   pl.BlockSpec((tm, V7X_LANES), lambda i: (i, 0))],
        compiler_params=_params(1, 2 * tm * d * (4 + 2) + 4 * tm * d * 4),
        name="prenorm",
    )(x, gains)


def _emit_residual(res, rest, with_next):
    if not with_next:
        (o_ref,) = rest
        o_ref[...] = res
        return
    g_ref, o_ref, xg_ref, ssq_ref = rest
    o_ref[...] = res
    xg_ref[...] = (res * g_ref[...]).astype(xg_ref.dtype)
    part = _lane_partial_sumsq(res)
    j = pl.program_id(1)

    @pl.when(j == 0)
    def _():
        ssq_ref[...] = part

    @pl.when(j != 0)
    def _():
        ssq_ref[...] += part


def _residual_specs(m, n, tm, tn, next_norm):
    out_shape = [jax.ShapeDtypeStruct((m, n), F32)]
    out_specs = [pl.BlockSpec((tm, tn), lambda i, j: (i, j))]
    if next_norm is None:
        return [], [], out_shape[0], out_specs[0]
    gains, layer = next_norm
    out_shape += [jax.ShapeDtypeStruct((m, n), BF16), jax.ShapeDtypeStruct((m, V7X_LANES), F32)]
    out_specs += [pl.BlockSpec((tm, tn), lambda i, j: (i, j)),
                  pl.BlockSpec((tm, V7X_LANES), lambda i, j: (i, 0))]
    return [pl.BlockSpec((None, 1, tn), lambda i, j: (layer, 0, j))], [gains], out_shape, out_specs


def _norm_rope(x, g, cos, sin):
    ms = jnp.mean(x * x, axis=-1, keepdims=True)
    y = x * lax.rsqrt(ms + EPS) * g
    return y * cos + pltpu.roll(y, HEAD_DIM // 2, axis=1) * sin


def _inproj_kernel(flag_ref, a_ref, ssq_ref, b_ref, g_ref, cos_ref, sin_ref, o_ref):
    j = pl.program_id(1)
    tm, k = a_ref.shape

    @pl.when(flag_ref[j] == 1)
    def _():
        b = b_ref[...]
        accs = [(rows, jnp.dot(a_ref[rows, :], b, preferred_element_type=F32)) for rows in _row_parts(tm)]
        for rows, acc in accs:
            acc = acc * _inv_rms(ssq_ref, rows, k)
            cos = cos_ref[rows, :]
            sin = sin_ref[rows, :]
            for c in range(acc.shape[1] // HEAD_DIM):
                sl = slice(c * HEAD_DIM, (c + 1) * HEAD_DIM)
                o_ref[rows, sl] = _norm_rope(acc[:, sl], g_ref[:, sl], cos, sin).astype(o_ref.dtype)

    @pl.when(flag_ref[j] == 0)
    def _():
        acc = jnp.dot(a_ref[...], b_ref[...], preferred_element_type=F32)
        o_ref[...] = (acc * _inv_rms(ssq_ref, slice(None), k)).astype(o_ref.dtype)


def _inproj(xg, ssq, w, layer, gain_cols, flags, cos, sin, tn):
    m, k = xg.shape
    n = w.shape[2]
    s = cos.shape[0]
    tm = _largest_tile(s, 1024, 8)
    n_pos_blocks = s // tm
    vmem = 2 * (tm * k * 2 + k * tn * 2 + tm * tn * 2 + 2 * tm * HEAD_DIM * 4) + 3 * tm * tn * 4
    return pl.pallas_call(
        _inproj_kernel,
        out_shape=jax.ShapeDtypeStruct((m, n), BF16),
        grid_spec=pltpu.PrefetchScalarGridSpec(
            num_scalar_prefetch=1,
            grid=(m // tm, n // tn),
            in_specs=[pl.BlockSpec((tm, k), lambda i, j, f: (i, 0)),
                      pl.BlockSpec((tm, V7X_LANES), lambda i, j, f: (i, 0)),
                      pl.BlockSpec((None, k, tn), lambda i, j, f: (layer, 0, j)),
                      pl.BlockSpec((1, tn), lambda i, j, f: (0, j)),
                      pl.BlockSpec((tm, HEAD_DIM), lambda i, j, f: (i % n_pos_blocks, 0)),
                      pl.BlockSpec((tm, HEAD_DIM), lambda i, j, f: (i % n_pos_blocks, 0))],
            out_specs=pl.BlockSpec((tm, tn), lambda i, j, f: (i, j))),
        compiler_params=_params(2, vmem),
        name="inproj",
    )(flags, xg, ssq, w, gain_cols, cos, sin)


def _mm_res_kernel(a_ref, b_ref, r_ref, *rest, with_next):
    _emit_residual(r_ref[...] + jnp.dot(a_ref[...], b_ref[...], preferred_element_type=F32), rest, with_next)


def _mm_res(a, w, layer, r, next_norm, tm_cap, name):
    m, k = a.shape
    n = w.shape[2]
    tm = _largest_tile(m, tm_cap, 8)
    tn = _largest_tile(n, 512)
    extra_specs, extra_ops, out_shape, out_specs = _residual_specs(m, n, tm, tn, next_norm)
    vmem = 2 * (tm * k * 2 + k * tn * 2 + 2 * tm * tn * 4 + tm * tn * 2 + tm * V7X_LANES * 4) + 4 * tm * tn * 4
    return pl.pallas_call(
        functools.partial(_mm_res_kernel, with_next=next_norm is not None),
        out_shape=out_shape,
        grid=(m // tm, n // tn),
        in_specs=[pl.BlockSpec((tm, k), lambda i, j: (i, 0)),
                  pl.BlockSpec((None, k, tn), lambda i, j: (layer, 0, j)),
                  pl.BlockSpec((tm, tn), lambda i, j: (i, j))] + extra_specs,
        out_specs=out_specs,
        compiler_params=_params(2, vmem),
        name=name,
    )(a, w, r, *extra_ops)


def _mm2_res_kernel(a1_ref, b1_ref, a2_ref, b2_ref, r_ref, *rest, with_next):
    acc = jnp.dot(a1_ref[...], b1_ref[...], preferred_element_type=F32)
    acc += jnp.dot(a2_ref[...], b2_ref[...], preferred_element_type=F32)
    _emit_residual(r_ref[...] + acc, rest, with_next)


def _mm2_res(a1, a2, w, layer, r, next_norm):
    m, k1 = a1.shape
    k2 = a2.shape[1]
    assert k1 == k2, "the two head groups index w[layer] as row blocks 0 and 1"
    n = w.shape[2]
    tm = _largest_tile(m, 1024, 8)
    tn = _largest_tile(n, 512)
    extra_specs, extra_ops, out_shape, out_specs = _residual_specs(m, n, tm, tn, next_norm)
    vmem = (2 * (tm * (k1 + k2) * 2 + (k1 + k2) * tn * 2 + 2 * tm * tn * 4 + tm * tn * 2 + tm * V7X_LANES * 4)
            + 4 * tm * tn * 4)
    return pl.pallas_call(
        functools.partial(_mm2_res_kernel, with_next=next_norm is not None),
        out_shape=out_shape,
        grid=(m // tm, n // tn),
        in_specs=[pl.BlockSpec((tm, k1), lambda i, j: (i, 0)),
                  pl.BlockSpec((None, k1, tn), lambda i, j: (layer, 0, j)),
                  pl.BlockSpec((tm, k2), lambda i, j: (i, 0)),
                  pl.BlockSpec((None, k2, tn), lambda i, j: (layer, 1, j)),
                  pl.BlockSpec((tm, tn), lambda i, j: (i, j))] + extra_specs,
        out_specs=out_specs,
        compiler_params=_params(2, vmem),
        name="outproj2",
    )(a1, w, a2, w, r, *extra_ops)


def _gateup_kernel(a_ref, ssq_ref, wg_ref, wu_ref, o_ref):
    wg = wg_ref[...]
    wu = wu_ref[...]
    parts = []
    for rows in _row_parts(a_ref.shape[0]):
        a = a_ref[rows, :]
        parts.append((rows, jnp.dot(a, wg, preferred_element_type=F32), jnp.dot(a, wu, preferred_element_type=F32)))
    for rows, g, u in parts:
        inv = _inv_rms(ssq_ref, rows, a_ref.shape[1])
        e = jnp.exp2(g * (inv * -LOG2_E))
        o_ref[rows, :] = ((g * u) * ((inv * inv) / (1.0 + e))).astype(o_ref.dtype)


def _gateup(xg, ssq, wg, wu, layer):
    m, k = xg.shape
    n = wg.shape[2]
    tm = _largest_tile(m, 2048, 8)
    tn = _largest_tile(n, 512, V7X_MXU_COLS)
    vmem = 2 * (tm * k * 2 + 2 * k * tn * 2 + tm * tn * 2) + 6 * tm * tn * 4
    return pl.pallas_call(
        _gateup_kernel,
        out_shape=jax.ShapeDtypeStruct((m, n), BF16),
        grid=(m // tm, n // tn),
        in_specs=[pl.BlockSpec((tm, k), lambda i, j: (i, 0)),
                  pl.BlockSpec((tm, V7X_LANES), lambda i, j: (i, 0)),
                  pl.BlockSpec((None, k, tn), lambda i, j: (layer, 0, j)),
                  pl.BlockSpec((None, k, tn), lambda i, j: (layer, 0, j))],
        out_specs=pl.BlockSpec((tm, tn), lambda i, j: (i, j)),
        compiler_params=_params(2, vmem),
        name="gateup",
    )(xg, ssq, wg, wu)


DIFF_ROW_CHUNK = 256


def _diff_attn_kernel(q_ref, k_ref, v_ref, lam_ref, subln_ref, o_ref, *, lam_init):
    q = q_ref[...]
    k = k_ref[...]
    v = v_ref[...]
    lf = lam_ref[...]
    lam = (jnp.exp(jnp.sum(lf[0:1] * lf[1:2], axis=-1, keepdims=True))
           - jnp.exp(jnp.sum(lf[2:3] * lf[3:4], axis=-1, keepdims=True)) + lam_init)

    halves = (slice(0, HEAD_DIM), slice(HEAD_DIM, 2 * HEAD_DIM))
    tq = q.shape[0]
    chunk = math.gcd(tq, DIFF_ROW_CHUNK)
    rows = [slice(r0, r0 + chunk) for r0 in range(0, tq, chunk)]

    def scores(rs):
        return [lax.dot_general(q[rs, c], k[:, c], NT_DIMS, preferred_element_type=F32) for c in halves]

    def finish(ss):
        probs = []
        for s in ss:
            p = jnp.exp2(s - jnp.max(s, axis=-1, keepdims=True))
            probs.append((p.astype(BF16), jnp.sum(p, axis=-1, keepdims=True)))
        o1, o2 = [jnp.dot(p, v, preferred_element_type=F32) / l for p, l in probs]
        return o1 - lam * o2

    pending = scores(rows[0])
    outs = []
    for nxt in rows[1:]:
        ahead = scores(nxt)
        outs.append(finish(pending))
        pending = ahead
    outs.append(finish(pending))
    o = jnp.concatenate(outs, axis=0)
    ms = jnp.mean(o * o, axis=-1, keepdims=True)
    y = o * lax.rsqrt(ms + EPS) * subln_ref[...]
    o_ref[...] = (y * (1.0 - lam_init)).astype(o_ref.dtype)


def _diff_attn(proj, lambdas, subln, layer, batch, seq, a_width, lam_init):
    m = proj.shape[0]
    hd2 = 2 * HEAD_DIM
    n_heads = a_width // hd2
    tq = _largest_tile(seq, 1024, 8)
    nq = seq // tq
    vmem = 2 * (2 * tq * hd2 * 2 + 2 * seq * hd2 * 2) + 6 * tq * seq * 4
    return pl.pallas_call(
        functools.partial(_diff_attn_kernel, lam_init=lam_init),
        out_shape=jax.ShapeDtypeStruct((m, a_width), BF16),
        grid=(batch, n_heads, nq),
        in_specs=[pl.BlockSpec((tq, hd2), lambda b, h, t: (b * nq + t, h)),
                  pl.BlockSpec((seq, hd2), lambda b, h, t: (b, n_heads + h)),
                  pl.BlockSpec((seq, hd2), lambda b, h, t: (b, 2 * n_heads + h)),
                  pl.BlockSpec((None, 4, HEAD_DIM), lambda b, h, t: (layer, 0, 0)),
                  pl.BlockSpec((None, 1, hd2), lambda b, h, t: (layer, 0, 0))],
        out_specs=pl.BlockSpec((tq, hd2), lambda b, h, t: (b * nq + t, h)),
        compiler_params=_params(3, vmem),
        name="diff_attn",
    )(proj, proj, proj, lambdas, subln)


BAND_TQ = 128


def _band_geometry(n, half_width):
    tq = min(BAND_TQ, n)
    kw = min(n, tq + 2 * half_width)
    return tq, kw, n // tq


def _band_key_start(q0, n, kw, half_width):
    return pl.multiple_of(jnp.clip(q0 - half_width, 0, n - kw), half_width)


def _band_mask(tq, kw, k0_minus_q0, half_width):
    rel = (lax.broadcasted_iota(jnp.int32, (tq, kw), 1) + k0_minus_q0
           - lax.broadcasted_iota(jnp.int32, (tq, kw), 0))
    return jnp.abs(rel) <= half_width


def _band_tiles(tiles, sinks=None):
    sinks = [None] * len(tiles) if sinks is None else sinks
    scores = [lax.dot_general(q, k, NT_DIMS, preferred_element_type=F32) for (q, k, _, _) in tiles]
    probs = []
    for s, (_, _, _, valid), sk in zip(scores, tiles, sinks):
        s = jnp.where(valid, s, NEG_INF)
        m = jnp.max(s, axis=-1, keepdims=True)
        if sk is not None:
            m = jnp.maximum(m, sk)
        p = jnp.exp2(s - m)
        l = jnp.sum(p, axis=-1, keepdims=True)
        if sk is not None:
            l = l + jnp.exp2(sk - m)
        probs.append((p.astype(BF16), m, l))
    return [(jnp.dot(p, v, preferred_element_type=F32) / l, m + jnp.log2(l))
            for (p, m, l), (_, _, v, _) in zip(probs, tiles)]


def _fold(dst_ref, src_ref, factor, n_blocks, cast=None):
    n_tensors, rows = src_ref.shape[0], src_ref.shape[1]
    src_block = rows // n_blocks
    dst_block = src_block // factor
    for i in range(n_tensors):
        for j in range(n_blocks):
            for c in range(factor):
                v = src_ref[i, pl.ds(j * src_block + c, dst_block, stride=factor), :]
                dst_ref[i, (j * factor + c) * dst_block:(j * factor + c + 1) * dst_block, :] = (
                    v if cast is None else v.astype(cast))


def _unfold(dst_ref, src_ref, factor, n_blocks):
    n_tensors, rows = src_ref.shape[0], src_ref.shape[1]
    dst_block = rows // n_blocks
    src_block = dst_block // factor
    for i in range(n_tensors):
        for j in range(n_blocks):
            for c in range(factor):
                dst_ref[i, pl.ds(j * dst_block + c, src_block, stride=factor), :] = (
                    src_ref[i, (j * factor + c) * src_block:(j * factor + c + 1) * src_block, :])


def _dilated_kernel(q_ref, k_ref, v_ref, o_ref, nat_ref, x4_ref, x16_ref, r16_ref, u4_ref, r4_ref):
    seq = q_ref.shape[0]
    (w1, d1), (w4, d4), (w16, d16) = DIL_PATTERNS
    assert (d1, d4, d16) == (1, 4, 16) and w1 // (2 * d1) == w4 // (2 * d4) == w16 // (2 * d16)
    hw = w1 // 2

    for i, ref in enumerate((q_ref, k_ref, v_ref)):
        nat_ref[i] = ref[...].astype(F32)
    _fold(x4_ref, nat_ref, 4, 1)
    _fold(x16_ref, x4_ref, 4, 4, cast=BF16)

    sub16 = seq // 16
    tq, kw, n_tiles = _band_geometry(sub16, hw)
    assert n_tiles == 1 and kw == sub16
    valid16 = _band_mask(tq, kw, 0, hw)
    blocks = [slice(r * sub16, (r + 1) * sub16) for r in range(16)]
    tiles = [(x16_ref[0, rows, :], x16_ref[1, rows, :], x16_ref[2, rows, :], valid16) for rows in blocks]
    for rows, (o, lse) in zip(blocks, _band_tiles(tiles)):
        r16_ref[0, rows, :] = o
        r16_ref[1, rows, :] = jnp.broadcast_to(lse, (sub16, HEAD_DIM))
    _unfold(u4_ref, r16_ref, 4, 4)

    sub4 = seq // 4
    tq, kw, n_tiles = _band_geometry(sub4, hw)
    blocks, tiles = [], []
    for c in range(4):
        for t in range(n_tiles):
            q0 = t * tq
            k0 = min(max(q0 - hw, 0), sub4 - kw)
            qrows = slice(c * sub4 + q0, c * sub4 + q0 + tq)
            krows = slice(c * sub4 + k0, c * sub4 + k0 + kw)
            blocks.append(qrows)
            tiles.append((x4_ref[0, qrows, :].astype(BF16), x4_ref[1, krows, :].astype(BF16),
                          x4_ref[2, krows, :].astype(BF16), _band_mask(tq, kw, k0 - q0, hw)))
    for qrows, (o, lse) in zip(blocks, _band_tiles(tiles)):
        r4_ref[0, qrows, :] = o
        r4_ref[1, qrows, :] = jnp.broadcast_to(lse, (tq, HEAD_DIM))
    _unfold(r16_ref, u4_ref, 4, 1)
    _unfold(nat_ref, r4_ref, 4, 1)

    tq, kw, n_tiles = _band_geometry(seq, hw)
    blocks, tiles = [], []
    for t in range(n_tiles):
        q0 = t * tq
        k0 = min(max(q0 - hw, 0), seq - kw)
        qrows, krows = slice(q0, q0 + tq), slice(k0, k0 + kw)
        blocks.append(qrows)
        tiles.append((q_ref[qrows, :], k_ref[krows, :], v_ref[krows, :], _band_mask(tq, kw, k0 - q0, hw)))
    for qrows, (o, lse) in zip(blocks, _band_tiles(tiles)):
        lses = [lse, nat_ref[1, qrows, :], r16_ref[1, qrows, :]]
        outs = [o, nat_ref[0, qrows, :], r16_ref[0, qrows, :]]
        top = functools.reduce(jnp.maximum, lses)
        ws = [jnp.exp2(x - top) for x in lses]
        num = functools.reduce(lambda a, b: a + b, [w * x for w, x in zip(ws, outs)])
        den = functools.reduce(lambda a, b: a + b, ws)
        o_ref[qrows, :] = (num / den).astype(o_ref.dtype)


def _dilated_attn(proj, batch, seq, col0, b_width):
    m = proj.shape[0]
    n_heads = b_width // HEAD_DIM
    c0 = col0 // HEAD_DIM
    plane = seq * HEAD_DIM
    scratch_bytes = (3 + 3 + 2 + 2 + 2) * plane * 4 + 3 * plane * 2
    vmem = 2 * 4 * plane * 2 + scratch_bytes + 12 * 1024 * 1024
    spec = lambda off: pl.BlockSpec((seq, HEAD_DIM), lambda b, h: (b, c0 + off + h))
    return pl.pallas_call(
        _dilated_kernel,
        out_shape=jax.ShapeDtypeStruct((m, b_width), BF16),
        grid=(batch, n_heads),
        in_specs=[spec(0), spec(n_heads), spec(2 * n_heads)],
        out_specs=pl.BlockSpec((seq, HEAD_DIM), lambda b, h: (b, h)),
        scratch_shapes=[pltpu.VMEM((3, seq, HEAD_DIM), F32), pltpu.VMEM((3, seq, HEAD_DIM), F32),
                        pltpu.VMEM((3, seq, HEAD_DIM), BF16), pltpu.VMEM((2, seq, HEAD_DIM), F32),
                        pltpu.VMEM((2, seq, HEAD_DIM), F32), pltpu.VMEM((2, seq, HEAD_DIM), F32)],
        compiler_params=_params(2, vmem),
        name="dilated_attn",
    )(proj, proj, proj)


WIN_TILES_PER_STEP = 4


def _window_kernel(sink_ref, q_ref, k_ref, v_ref, o_ref, *, group, layer):
    seq = k_ref.shape[0]
    hk = pl.program_id(1)
    tq, kw, n_tiles = _band_geometry(seq, WIN_HALF)
    per_step = math.gcd(n_tiles, WIN_TILES_PER_STEP)

    def tiles_body(step, carry):
        dests, tiles, sinks = [], [], []
        for tt in range(per_step):
            q0 = pl.multiple_of((step * per_step + tt) * tq, tq)
            k0 = _band_key_start(q0, seq, kw, WIN_HALF)
            qrows, krows = pl.ds(q0, tq), pl.ds(k0, kw)
            k = k_ref[krows, :]
            v = v_ref[krows, :]
            valid = _band_mask(tq, kw, k0 - q0, WIN_HALF)
            for g in range(group):
                cols = slice(g * HEAD_DIM, (g + 1) * HEAD_DIM)
                dests.append((qrows, cols))
                tiles.append((q_ref[qrows, cols], k, v, valid))
                sinks.append(sink_ref[layer, hk * group + g] * LOG2_E)
        for (qrows, cols), (o, _) in zip(dests, _band_tiles(tiles, sinks)):
            o_ref[qrows, cols] = o.astype(o_ref.dtype)
        return carry

    lax.fori_loop(0, n_tiles // per_step, tiles_body, 0)


def _window_attn(proj, sinks, layer, batch, seq, n_q_heads, n_kv_heads):
    m = proj.shape[0]
    group = n_q_heads // n_kv_heads
    gw = group * HEAD_DIM
    vmem = 2 * (2 * seq * gw * 2 + 2 * seq * HEAD_DIM * 2) + 16 * 1024 * 1024
    return pl.pallas_call(
        functools.partial(_window_kernel, group=group, layer=layer),
        out_shape=jax.ShapeDtypeStruct((m, n_q_heads * HEAD_DIM), BF16),
        grid_spec=pltpu.PrefetchScalarGridSpec(
            num_scalar_prefetch=1,
            grid=(batch, n_kv_heads),
            in_specs=[pl.BlockSpec((seq, gw), lambda b, h, s: (b, h)),
                      pl.BlockSpec((seq, HEAD_DIM), lambda b, h, s: (b, n_q_heads + h)),
                      pl.BlockSpec((seq, HEAD_DIM), lambda b, h, s: (b, n_q_heads + n_kv_heads + h))],
            out_specs=pl.BlockSpec((seq, gw), lambda b, h, s: (b, h))),
        compiler_params=_params(2, vmem),
        name="window_attn",
    )(sinks, proj, proj, proj)


def _rope_tables(seq):
    inv_freq = ROPE_THETA ** (-jnp.arange(0, HEAD_DIM, 2, dtype=F32) / HEAD_DIM)
    ang = jnp.arange(seq, dtype=F32)[:, None] * inv_freq[None, :]
    ang = jnp.concatenate([ang, ang], axis=-1)
    sign = jnp.where(jnp.arange(HEAD_DIM) < HEAD_DIM // 2, -1.0, 1.0).astype(F32)
    return jnp.cos(ang), jnp.sin(ang) * sign


def _section_gains(sections):
    cols, flags = [], []
    for width, g in sections:
        reps = width // HEAD_DIM
        if g is None:
            cols.append(jnp.ones((width,), F32))
            flags += [0] * reps
        else:
            cols.append(jnp.tile(g.astype(F32), reps))
            flags += [1] * reps
    return jnp.concatenate(cols).reshape(1, -1), flags


def _tile_flags(flags, tn):
    per = tn // HEAD_DIM
    tiles = [flags[i:i + per] for i in range(0, len(flags), per)]
    assert all(len(set(t)) == 1 for t in tiles), "a column tile mixes normed and plain heads"
    return jnp.asarray([t[0] for t in tiles], jnp.int32)


def kernel(x, mix_norm, ffn_norm, w_gate, w_up, w_down, hy_w_in, hy_w_out, diff_q_norm, diff_k_norm,
           diff_lambda, diff_subln, dil_q_norm, dil_k_norm, win_w_in, win_w_out, win_q_norm, win_k_norm,
           win_sink):
    batch, seq, d_model = x.shape
    depth = mix_norm.shape[0]
    a_width = d_model // 2
    b_width = d_model - a_width
    n_q_heads = d_model // HEAD_DIM
    n_kv_heads = (win_w_in.shape[-1] // HEAD_DIM - n_q_heads) // 2
    scale = HEAD_DIM ** -0.5 * LOG2_E

    cos, sin = _rope_tables(seq)
    xf = x.reshape(batch * seq, d_model)
    mix_gain = mix_norm.astype(F32)[:, None, :]
    ffn_gain = ffn_norm.astype(F32)[:, None, :]
    w_gate, w_up, w_down = (w.astype(BF16) for w in (w_gate, w_up, w_down))
    hy_w_in, hy_w_out, win_w_in, win_w_out = (w.astype(BF16) for w in (hy_w_in, hy_w_out, win_w_in, win_w_out))
    diff_lambda = diff_lambda.astype(F32)
    diff_subln = diff_subln.astype(F32)[:, None, :]
    win_sink = win_sink.astype(F32)

    xg, ssq = _prenorm(xf, mix_gain, 0)
    for layer in range(depth):
        ffn_norm_next = (ffn_gain, layer)
        if layer % 2 == 0:
            e = layer // 2
            gains, flags = _section_gains([
                (a_width, diff_q_norm[e] * scale), (a_width, diff_k_norm[e]), (a_width, None),
                (b_width, dil_q_norm[e] * scale), (b_width, dil_k_norm[e]), (b_width, None)])
            tn = _largest_tile(math.gcd(a_width, b_width), 1024)
            proj = _inproj(xg, ssq, hy_w_in, e, gains, _tile_flags(flags, tn), cos, sin, tn)
            lam_init = 0.8 - 0.6 * math.exp(-0.3 * layer)
            ao = _diff_attn(proj, diff_lambda, diff_subln, e, batch, seq, a_width, lam_init)
            bo = _dilated_attn(proj, batch, seq, 3 * a_width, b_width)
            xf, xg, ssq = _mm2_res(ao, bo, hy_w_out, e, xf, ffn_norm_next)
        else:
            o = layer // 2
            qd, kd = n_q_heads * HEAD_DIM, n_kv_heads * HEAD_DIM
            gains, flags = _section_gains([
                (qd, win_q_norm[o] * scale), (kd, win_k_norm[o]), (kd, None)])
            tn = _largest_tile(math.gcd(qd, kd), 1024)
            proj = _inproj(xg, ssq, win_w_in, o, gains, _tile_flags(flags, tn), cos, sin, tn)
            att = _window_attn(proj, win_sink, o, batch, seq, n_q_heads, n_kv_heads)
            xf, xg, ssq = _mm_res(att, win_w_out, o, xf, ffn_norm_next, 1024, "outproj")
        hidden = _gateup(xg, ssq, w_gate, w_up, layer)
        if layer + 1 < depth:
            xf, xg, ssq = _mm_res(hidden, w_down, layer, xf, (mix_gain, layer + 1), 512, "down")
        else:
            xf = _mm_res(hidden, w_down, layer, xf, None, 512, "down")
    return xf.reshape(batch, seq, d_model)
```

```python
import functools
import math

import jax
import jax.numpy as jnp
from jax import lax
from jax.experimental import pallas as pl
from jax.experimental.pallas import tpu as pltpu

HEAD_DIM = 128
DIL_PATTERNS = ((128, 1), (512, 4), (2048, 16))
WIN_HALF = 128
ROPE_THETA = 10000.0
EPS = 1e-6
NEG_INF = -1e30
LOG2_E = 1.4426950408889634

V7X_LANES = 128
V7X_MXU_COLS = 256
V7X_VMEM_LIMIT_CAP = 56 * 1024 * 1024

F32 = jnp.float32
BF16 = jnp.bfloat16
NT_DIMS = (((1,), (1,)), ((), ()))


def _params(n_grid_dims, vmem_bytes):
    return pltpu.CompilerParams(
        dimension_semantics=("arbitrary",) * n_grid_dims,
        vmem_limit_bytes=int(min(vmem_bytes, V7X_VMEM_LIMIT_CAP)))


def _largest_tile(n, cap, quantum=V7X_LANES):
    best = None
    t = quantum
    while t <= min(n, cap):
        if n % t == 0:
            best = t
        t += quantum
    assert best is not None, (n, cap)
    return best


def _lane_partial_sumsq(x):
    sq = x * x
    return functools.reduce(lambda a, b: a + b,
                            [sq[:, c:c + V7X_LANES] for c in range(0, x.shape[1], V7X_LANES)])


def _inv_rms(ssq_ref, rows, d_model):
    return lax.rsqrt(jnp.sum(ssq_ref[rows, :], axis=-1, keepdims=True) * (1.0 / d_model) + EPS)


ROW_PARTS = 4


def _row_parts(tm):
    part = tm // ROW_PARTS
    return [slice(r0, r0 + part) for r0 in range(0, tm, part)]


def _prenorm_kernel(x_ref, g_ref, xg_ref, ssq_ref):
    x = x_ref[...]
    xg_ref[...] = (x * g_ref[...]).astype(xg_ref.dtype)
    ssq_ref[...] = _lane_partial_sumsq(x)


def _prenorm(x, gains, layer):
    m, d = x.shape
    tm = _largest_tile(m, 256, 8)
    return pl.pallas_call(
        _prenorm_kernel,
        out_shape=[jax.ShapeDtypeStruct((m, d), BF16), jax.ShapeDtypeStruct((m, V7X_LANES), F32)],
        grid=(m // tm,),
        in_specs=[pl.BlockSpec((tm, d), lambda i: (i, 0)),
                  pl.BlockSpec((None, 1, d), lambda i: (layer, 0, 0))],
        out_specs=[pl.BlockSpec((tm, d), lambda i: (i, 0)),
                   pl.BlockSpec((tm, V7X_LANES), lambda i: (i, 0))],
        compiler_params=_params(1, 2 * tm * d * (4 + 2) + 4 * tm * d * 4),
        name="prenorm",
    )(x, gains)


def _emit_residual(res, rest, with_next):
    if not with_next:
        (o_ref,) = rest
        o_ref[...] = res
        return
    g_ref, o_ref, xg_ref, ssq_ref = rest
    o_ref[...] = res
    xg_ref[...] = (res * g_ref[...]).astype(xg_ref.dtype)
    part = _lane_partial_sumsq(res)
    j = pl.program_id(1)

    @pl.when(j == 0)
    def _():
        ssq_ref[...] = part

    @pl.when(j != 0)
    def _():
        ssq_ref[...] += part


def _residual_specs(m, n, tm, tn, next_norm):
    out_shape = [jax.ShapeDtypeStruct((m, n), F32)]
    out_specs = [pl.BlockSpec((tm, tn), lambda i, j: (i, j))]
    if next_norm is None:
        return [], [], out_shape[0], out_specs[0]
    gains, layer = next_norm
    out_shape += [jax.ShapeDtypeStruct((m, n), BF16), jax.ShapeDtypeStruct((m, V7X_LANES), F32)]
    out_specs += [pl.BlockSpec((tm, tn), lambda i, j: (i, j)),
                  pl.BlockSpec((tm, V7X_LANES), lambda i, j: (i, 0))]
    return [pl.BlockSpec((None, 1, tn), lambda i, j: (layer, 0, j))], [gains], out_shape, out_specs


def _norm_rope(x, g, cos, sin):
    ms = jnp.mean(x * x, axis=-1, keepdims=True)
    y = x * lax.rsqrt(ms + EPS) * g
    return y * cos + pltpu.roll(y, HEAD_DIM // 2, axis=1) * sin


def _inproj_kernel(flag_ref, a_ref, ssq_ref, b_ref, g_ref, cos_ref, sin_ref, o_ref):
    j = pl.program_id(1)
    tm, k = a_ref.shape

    @pl.when(flag_ref[j] == 1)
    def _():
        b = b_ref[...]
        accs = [(rows, jnp.dot(a_ref[rows, :], b, preferred_element_type=F32)) for rows in _row_parts(tm)]
        for rows, acc in accs:
            acc = acc * _inv_rms(ssq_ref, rows, k)
            cos = cos_ref[rows, :]
            sin = sin_ref[rows, :]
            for c in range(acc.shape[1] // HEAD_DIM):
                sl = slice(c * HEAD_DIM, (c + 1) * HEAD_DIM)
                o_ref[rows, sl] = _norm_rope(acc[:, sl], g_ref[:, sl], cos, sin).astype(o_ref.dtype)

    @pl.when(flag_ref[j] == 0)
    def _():
        acc = jnp.dot(a_ref[...], b_ref[...], preferred_element_type=F32)
        o_ref[...] = (acc * _inv_rms(ssq_ref, slice(None), k)).astype(o_ref.dtype)


def _inproj(xg, ssq, w, layer, gain_cols, flags, cos, sin, tn):
    m, k = xg.shape
    n = w.shape[2]
    s = cos.shape[0]
    tm = _largest_tile(s, 1024, 8)
    n_pos_blocks = s // tm
    vmem = 2 * (tm * k * 2 + k * tn * 2 + tm * tn * 2 + 2 * tm * HEAD_DIM * 4) + 3 * tm * tn * 4
    return pl.pallas_call(
        _inproj_kernel,
        out_shape=jax.ShapeDtypeStruct((m, n), BF16),
        grid_spec=pltpu.PrefetchScalarGridSpec(
            num_scalar_prefetch=1,
            grid=(m // tm, n // tn),
            in_specs=[pl.BlockSpec((tm, k), lambda i, j, f: (i, 0)),
                      pl.BlockSpec((tm, V7X_LANES), lambda i, j, f: (i, 0)),
                      pl.BlockSpec((None, k, tn), lambda i, j, f: (layer, 0, j)),
                      pl.BlockSpec((1, tn), lambda i, j, f: (0, j)),
                      pl.BlockSpec((tm, HEAD_DIM), lambda i, j, f: (i % n_pos_blocks, 0)),
                      pl.BlockSpec((tm, HEAD_DIM), lambda i, j, f: (i % n_pos_blocks, 0))],
            out_specs=pl.BlockSpec((tm, tn), lambda i, j, f: (i, j))),
        compiler_params=_params(2, vmem),
        name="inproj",
    )(flags, xg, ssq, w, gain_cols, cos, sin)


def _mm_res_kernel(a_ref, b_ref, r_ref, *rest, with_next):
    _emit_residual(r_ref[...] + jnp.dot(a_ref[...], b_ref[...], preferred_element_type=F32), rest, with_next)


def _mm_res(a, w, layer, r, next_norm, tm_cap, name):
    m, k = a.shape
    n = w.shape[2]
    tm = _largest_tile(m, tm_cap, 8)
    tn = _largest_tile(n, 512)
    extra_specs, extra_ops, out_shape, out_specs = _residual_specs(m, n, tm, tn, next_norm)
    vmem = 2 * (tm * k * 2 + k * tn * 2 + 2 * tm * tn * 4 + tm * tn * 2 + tm * V7X_LANES * 4) + 4 * tm * tn * 4
    return pl.pallas_call(
        functools.partial(_mm_res_kernel, with_next=next_norm is not None),
        out_shape=out_shape,
        grid=(m // tm, n // tn),
        in_specs=[pl.BlockSpec((tm, k), lambda i, j: (i, 0)),
                  pl.BlockSpec((None, k, tn), lambda i, j: (layer, 0, j)),
                  pl.BlockSpec((tm, tn), lambda i, j: (i, j))] + extra_specs,
        out_specs=out_specs,
        compiler_params=_params(2, vmem),
        name=name,
    )(a, w, r, *extra_ops)


def _mm2_res_kernel(a1_ref, b1_ref, a2_ref, b2_ref, r_ref, *rest, with_next):
    acc = jnp.dot(a1_ref[...], b1_ref[...], preferred_element_type=F32)
    acc += jnp.dot(a2_ref[...], b2_ref[...], preferred_element_type=F32)
    _emit_residual(r_ref[...] + acc, rest, with_next)


def _mm2_res(a1, a2, w, layer, r, next_norm):
    m, k1 = a1.shape
    k2 = a2.shape[1]
    assert k1 == k2, "the two head groups index w[layer] as row blocks 0 and 1"
    n = w.shape[2]
    tm = _largest_tile(m, 1024, 8)
    tn = _largest_tile(n, 512)
    extra_specs, extra_ops, out_shape, out_specs = _residual_specs(m, n, tm, tn, next_norm)
    vmem = (2 * (tm * (k1 + k2) * 2 + (k1 + k2) * tn * 2 + 2 * tm * tn * 4 + tm * tn * 2 + tm * V7X_LANES * 4)
            + 4 * tm * tn * 4)
    return pl.pallas_call(
        functools.partial(_mm2_res_kernel, with_next=next_norm is not None),
        out_shape=out_shape,
        grid=(m // tm, n // tn),
        in_specs=[pl.BlockSpec((tm, k1), lambda i, j: (i, 0)),
                  pl.BlockSpec((None, k1, tn), lambda i, j: (layer, 0, j)),
                  pl.BlockSpec((tm, k2), lambda i, j: (i, 0)),
                  pl.BlockSpec((None, k2, tn), lambda i, j: (layer, 1, j)),
                  pl.BlockSpec((tm, tn), lambda i, j: (i, j))] + extra_specs,
        out_specs=out_specs,
        compiler_params=_params(2, vmem),
        name="outproj2",
    )(a1, w, a2, w, r, *extra_ops)


def _gateup_kernel(a_ref, ssq_ref, wg_ref, wu_ref, o_ref):
    wg = wg_ref[...].astype(BF16)
    wu = wu_ref[...].astype(BF16)
    parts = []
    for rows in _row_parts(a_ref.shape[0]):
        a = a_ref[rows, :]
        parts.append((rows, jnp.dot(a, wg, preferred_element_type=F32), jnp.dot(a, wu, preferred_element_type=F32)))
    for rows, g, u in parts:
        inv = _inv_rms(ssq_ref, rows, a_ref.shape[1])
        e = jnp.exp2(g * (inv * -LOG2_E))
        o_ref[rows, :] = ((g * u) * ((inv * inv) / (1.0 + e))).astype(o_ref.dtype)


def _gateup(xg, ssq, wg, wu, layer):
    m, k = xg.shape
    n = wg.shape[2]
    tm = _largest_tile(m, 2048, 8)
    tn = _largest_tile(n, 512, V7X_MXU_COLS)
    w_bytes = wg.dtype.itemsize
    vmem = tm * k * 2 + 2 * (2 * k * tn * w_bytes + tm * tn * 2) + 2 * k * tn * 2 + 6 * tm * tn * 4
    return pl.pallas_call(
        _gateup_kernel,
        out_shape=jax.ShapeDtypeStruct((m, n), BF16),
        grid=(m // tm, n // tn),
        in_specs=[pl.BlockSpec((tm, k), lambda i, j: (i, 0), pipeline_mode=pl.Buffered(1)),
                  pl.BlockSpec((tm, V7X_LANES), lambda i, j: (i, 0)),
                  pl.BlockSpec((None, k, tn), lambda i, j: (layer, 0, j)),
                  pl.BlockSpec((None, k, tn), lambda i, j: (layer, 0, j))],
        out_specs=pl.BlockSpec((tm, tn), lambda i, j: (i, j)),
        compiler_params=_params(2, vmem),
        name="gateup",
    )(xg, ssq, wg, wu)


DIFF_ROW_CHUNK = 256


def _diff_attn_kernel(q_ref, k_ref, v_ref, lam_ref, subln_ref, o_ref, *, lam_init):
    q = q_ref[...]
    k = k_ref[...]
    v = v_ref[...]
    lf = lam_ref[...]
    lam = (jnp.exp(jnp.sum(lf[0:1] * lf[1:2], axis=-1, keepdims=True))
           - jnp.exp(jnp.sum(lf[2:3] * lf[3:4], axis=-1, keepdims=True)) + lam_init)

    halves = (slice(0, HEAD_DIM), slice(HEAD_DIM, 2 * HEAD_DIM))
    tq = q.shape[0]
    chunk = math.gcd(tq, DIFF_ROW_CHUNK)
    rows = [slice(r0, r0 + chunk) for r0 in range(0, tq, chunk)]

    def scores(rs):
        return [lax.dot_general(q[rs, c], k[:, c], NT_DIMS, preferred_element_type=F32) for c in halves]

    def finish(ss):
        ps = []
        for s in ss:
            p = jnp.exp2(s - jnp.max(s, axis=-1, keepdims=True))
            ps.append((p, 1.0 / jnp.sum(p, axis=-1, keepdims=True)))
        (p1, r1), (p2, r2) = ps
        a = p1 * r1 - p2 * (lam * r2)
        return jnp.dot(a.astype(BF16), v, preferred_element_type=F32)

    pending = scores(rows[0])
    outs = []
    for nxt in rows[1:]:
        ahead = scores(nxt)
        outs.append(finish(pending))
        pending = ahead
    outs.append(finish(pending))
    o = jnp.concatenate(outs, axis=0)
    ms = jnp.mean(o * o, axis=-1, keepdims=True)
    y = o * lax.rsqrt(ms + EPS) * subln_ref[...]
    o_ref[...] = (y * (1.0 - lam_init)).astype(o_ref.dtype)


def _diff_attn(proj, lambdas, subln, layer, batch, seq, a_width, lam_init):
    m = proj.shape[0]
    hd2 = 2 * HEAD_DIM
    n_heads = a_width // hd2
    tq = _largest_tile(seq, 1024, 8)
    nq = seq // tq
    vmem = 2 * (2 * tq * hd2 * 2 + 2 * seq * hd2 * 2) + 6 * tq * seq * 4
    return pl.pallas_call(
        functools.partial(_diff_attn_kernel, lam_init=lam_init),
        out_shape=jax.ShapeDtypeStruct((m, a_width), BF16),
        grid=(batch, n_heads, nq),
        in_specs=[pl.BlockSpec((tq, hd2), lambda b, h, t: (b * nq + t, h)),
                  pl.BlockSpec((seq, hd2), lambda b, h, t: (b, n_heads + h)),
                  pl.BlockSpec((seq, hd2), lambda b, h, t: (b, 2 * n_heads + h)),
                  pl.BlockSpec((None, 4, HEAD_DIM), lambda b, h, t: (layer, 0, 0)),
                  pl.BlockSpec((None, 1, hd2), lambda b, h, t: (layer, 0, 0))],
        out_specs=pl.BlockSpec((tq, hd2), lambda b, h, t: (b * nq + t, h)),
        compiler_params=_params(3, vmem),
        name="diff_attn",
    )(proj, proj, proj, lambdas, subln)


BAND_TQ = 128


def _band_geometry(n, half_width):
    tq = min(BAND_TQ, n)
    kw = min(n, tq + 2 * half_width)
    return tq, kw, n // tq


def _band_key_start(q0, n, kw, half_width):
    return pl.multiple_of(jnp.clip(q0 - half_width, 0, n - kw), half_width)


def _band_mask(tq, kw, k0_minus_q0, half_width):
    rel = (lax.broadcasted_iota(jnp.int32, (tq, kw), 1) + k0_minus_q0
           - lax.broadcasted_iota(jnp.int32, (tq, kw), 0))
    return jnp.abs(rel) <= half_width


def _band_tiles(tiles, sinks=None):
    sinks = [None] * len(tiles) if sinks is None else sinks
    scores = [lax.dot_general(q, k, NT_DIMS, preferred_element_type=F32) for (q, k, _, _) in tiles]
    probs = []
    for s, (_, _, _, valid), sk in zip(scores, tiles, sinks):
        s = jnp.where(valid, s, NEG_INF)
        m = jnp.max(s, axis=-1, keepdims=True)
        if sk is not None:
            m = jnp.maximum(m, sk)
        p = jnp.exp2(s - m)
        l = jnp.sum(p, axis=-1, keepdims=True)
        if sk is not None:
            l = l + jnp.exp2(sk - m)
        probs.append((p.astype(BF16), m, l))
    return [(jnp.dot(p, v, preferred_element_type=F32) / l, m + jnp.log2(l))
            for (p, m, l), (_, _, v, _) in zip(probs, tiles)]


def _fold(dst_ref, src_ref, factor, n_blocks, cast=None):
    n_tensors, rows = src_ref.shape[0], src_ref.shape[1]
    src_block = rows // n_blocks
    dst_block = src_block // factor
    for i in range(n_tensors):
        for j in range(n_blocks):
            for c in range(factor):
                v = src_ref[i, pl.ds(j * src_block + c, dst_block, stride=factor), :]
                dst_ref[i, (j * factor + c) * dst_block:(j * factor + c + 1) * dst_block, :] = (
                    v if cast is None else v.astype(cast))


def _unfold(dst_ref, src_ref, factor, n_blocks):
    n_tensors, rows = src_ref.shape[0], src_ref.shape[1]
    dst_block = rows // n_blocks
    src_block = dst_block // factor
    for i in range(n_tensors):
        for j in range(n_blocks):
            for c in range(factor):
                dst_ref[i, pl.ds(j * dst_block + c, src_block, stride=factor), :] = (
                    src_ref[i, (j * factor + c) * src_block:(j * factor + c + 1) * src_block, :])


def _dilated_kernel(q_ref, k_ref, v_ref, o_ref, nat_ref, x4_ref, x16_ref, r16_ref, u4_ref, r4_ref):
    seq = q_ref.shape[0]
    (w1, d1), (w4, d4), (w16, d16) = DIL_PATTERNS
    assert (d1, d4, d16) == (1, 4, 16) and w1 // (2 * d1) == w4 // (2 * d4) == w16 // (2 * d16)
    hw = w1 // 2

    for i, ref in enumerate((q_ref, k_ref, v_ref)):
        nat_ref[i] = ref[...].astype(F32)
    _fold(x4_ref, nat_ref, 4, 1)
    _fold(x16_ref, x4_ref, 4, 4, cast=BF16)

    sub16 = seq // 16
    tq, kw, n_tiles = _band_geometry(sub16, hw)
    assert n_tiles == 1 and kw == sub16
    valid16 = _band_mask(tq, kw, 0, hw)
    blocks = [slice(r * sub16, (r + 1) * sub16) for r in range(16)]
    tiles = [(x16_ref[0, rows, :], x16_ref[1, rows, :], x16_ref[2, rows, :], valid16) for rows in blocks]
    for rows, (o, lse) in zip(blocks, _band_tiles(tiles)):
        r16_ref[0, rows, :] = o
        r16_ref[1, rows, :] = jnp.broadcast_to(lse, (sub16, HEAD_DIM))
    _unfold(u4_ref, r16_ref, 4, 4)

    sub4 = seq // 4
    tq, kw, n_tiles = _band_geometry(sub4, hw)
    blocks, tiles = [], []
    for c in range(4):
        for t in range(n_tiles):
            q0 = t * tq
            k0 = min(max(q0 - hw, 0), sub4 - kw)
            qrows = slice(c * sub4 + q0, c * sub4 + q0 + tq)
            krows = slice(c * sub4 + k0, c * sub4 + k0 + kw)
            blocks.append(qrows)
            tiles.append((x4_ref[0, qrows, :].astype(BF16), x4_ref[1, krows, :].astype(BF16),
                          x4_ref[2, krows, :].astype(BF16), _band_mask(tq, kw, k0 - q0, hw)))
    for qrows, (o, lse) in zip(blocks, _band_tiles(tiles)):
        r4_ref[0, qrows, :] = o
        r4_ref[1, qrows, :] = jnp.broadcast_to(lse, (tq, HEAD_DIM))
    _unfold(r16_ref, u4_ref, 4, 1)
    _unfold(nat_ref, r4_ref, 4, 1)

    tq, kw, n_tiles = _band_geometry(seq, hw)
    blocks, tiles = [], []
    for t in range(n_tiles):
        q0 = t * tq
        k0 = min(max(q0 - hw, 0), seq - kw)
        qrows, krows = slice(q0, q0 + tq), slice(k0, k0 + kw)
        blocks.append(qrows)
        tiles.append((q_ref[qrows, :], k_ref[krows, :], v_ref[krows, :], _band_mask(tq, kw, k0 - q0, hw)))
    for qrows, (o, lse) in zip(blocks, _band_tiles(tiles)):
        lses = [lse, nat_ref[1, qrows, :], r16_ref[1, qrows, :]]
        outs = [o, nat_ref[0, qrows, :], r16_ref[0, qrows, :]]
        top = functools.reduce(jnp.maximum, lses)
        ws = [jnp.exp2(x - top) for x in lses]
        num = functools.reduce(lambda a, b: a + b, [w * x for w, x in zip(ws, outs)])
        den = functools.reduce(lambda a, b: a + b, ws)
        o_ref[qrows, :] = (num / den).astype(o_ref.dtype)


def _dilated_attn(proj, batch, seq, col0, b_width):
    m = proj.shape[0]
    n_heads = b_width // HEAD_DIM
    c0 = col0 // HEAD_DIM
    plane = seq * HEAD_DIM
    scratch_bytes = (3 + 3 + 2 + 2 + 2) * plane * 4 + 3 * plane * 2
    vmem = 2 * 4 * plane * 2 + scratch_bytes + 12 * 1024 * 1024
    spec = lambda off: pl.BlockSpec((seq, HEAD_DIM), lambda b, h: (b, c0 + off + h))
    return pl.pallas_call(
        _dilated_kernel,
        out_shape=jax.ShapeDtypeStruct((m, b_width), BF16),
        grid=(batch, n_heads),
        in_specs=[spec(0), spec(n_heads), spec(2 * n_heads)],
        out_specs=pl.BlockSpec((seq, HEAD_DIM), lambda b, h: (b, h)),
        scratch_shapes=[pltpu.VMEM((3, seq, HEAD_DIM), F32), pltpu.VMEM((3, seq, HEAD_DIM), F32),
                        pltpu.VMEM((3, seq, HEAD_DIM), BF16), pltpu.VMEM((2, seq, HEAD_DIM), F32),
                        pltpu.VMEM((2, seq, HEAD_DIM), F32), pltpu.VMEM((2, seq, HEAD_DIM), F32)],
        compiler_params=_params(2, vmem),
        name="dilated_attn",
    )(proj, proj, proj)


WIN_TILES_PER_STEP = 4


def _window_kernel(sink_ref, q_ref, k_ref, v_ref, o_ref, *, group, layer):
    seq = k_ref.shape[0]
    hk = pl.program_id(1)
    tq, kw, n_tiles = _band_geometry(seq, WIN_HALF)
    per_step = math.gcd(n_tiles, WIN_TILES_PER_STEP)

    def tiles_body(step, carry):
        dests, tiles, sinks = [], [], []
        for tt in range(per_step):
            q0 = pl.multiple_of((step * per_step + tt) * tq, tq)
            k0 = _band_key_start(q0, seq, kw, WIN_HALF)
            qrows, krows = pl.ds(q0, tq), pl.ds(k0, kw)
            k = k_ref[krows, :]
            v = v_ref[krows, :]
            valid = _band_mask(tq, kw, k0 - q0, WIN_HALF)
            for g in range(group):
                cols = slice(g * HEAD_DIM, (g + 1) * HEAD_DIM)
                dests.append((qrows, cols))
                tiles.append((q_ref[qrows, cols], k, v, valid))
                sinks.append(sink_ref[layer, hk * group + g] * LOG2_E)
        for (qrows, cols), (o, _) in zip(dests, _band_tiles(tiles, sinks)):
            o_ref[qrows, cols] = o.astype(o_ref.dtype)
        return carry

    lax.fori_loop(0, n_tiles // per_step, tiles_body, 0)


def _window_attn(proj, sinks, layer, batch, seq, n_q_heads, n_kv_heads):
    m = proj.shape[0]
    group = n_q_heads // n_kv_heads
    gw = group * HEAD_DIM
    vmem = 2 * (2 * seq * gw * 2 + 2 * seq * HEAD_DIM * 2) + 16 * 1024 * 1024
    return pl.pallas_call(
        functools.partial(_window_kernel, group=group, layer=layer),
        out_shape=jax.ShapeDtypeStruct((m, n_q_heads * HEAD_DIM), BF16),
        grid_spec=pltpu.PrefetchScalarGridSpec(
            num_scalar_prefetch=1,
            grid=(batch, n_kv_heads),
            in_specs=[pl.BlockSpec((seq, gw), lambda b, h, s: (b, h)),
                      pl.BlockSpec((seq, HEAD_DIM), lambda b, h, s: (b, n_q_heads + h)),
                      pl.BlockSpec((seq, HEAD_DIM), lambda b, h, s: (b, n_q_heads + n_kv_heads + h))],
            out_specs=pl.BlockSpec((seq, gw), lambda b, h, s: (b, h))),
        compiler_params=_params(2, vmem),
        name="window_attn",
    )(sinks, proj, proj, proj)


def _rope_tables(seq):
    inv_freq = ROPE_THETA ** (-jnp.arange(0, HEAD_DIM, 2, dtype=F32) / HEAD_DIM)
    ang = jnp.arange(seq, dtype=F32)[:, None] * inv_freq[None, :]
    ang = jnp.concatenate([ang, ang], axis=-1)
    sign = jnp.where(jnp.arange(HEAD_DIM) < HEAD_DIM // 2, -1.0, 1.0).astype(F32)
    return jnp.cos(ang), jnp.sin(ang) * sign


def _section_gains(sections):
    cols, flags = [], []
    for width, g in sections:
        reps = width // HEAD_DIM
        if g is None:
            cols.append(jnp.ones((width,), F32))
            flags += [0] * reps
        else:
            cols.append(jnp.tile(g.astype(F32), reps))
            flags += [1] * reps
    return jnp.concatenate(cols).reshape(1, -1), flags


def _tile_flags(flags, tn):
    per = tn // HEAD_DIM
    tiles = [flags[i:i + per] for i in range(0, len(flags), per)]
    assert all(len(set(t)) == 1 for t in tiles), "a column tile mixes normed and plain heads"
    return jnp.asarray([t[0] for t in tiles], jnp.int32)


def kernel(x, mix_norm, ffn_norm, w_gate, w_up, w_down, hy_w_in, hy_w_out, diff_q_norm, diff_k_norm,
           diff_lambda, diff_subln, dil_q_norm, dil_k_norm, win_w_in, win_w_out, win_q_norm, win_k_norm,
           win_sink):
    batch, seq, d_model = x.shape
    depth = mix_norm.shape[0]
    a_width = d_model // 2
    b_width = d_model - a_width
    n_q_heads = d_model // HEAD_DIM
    n_kv_heads = (win_w_in.shape[-1] // HEAD_DIM - n_q_heads) // 2
    scale = HEAD_DIM ** -0.5 * LOG2_E

    cos, sin = _rope_tables(seq)
    xf = x.reshape(batch * seq, d_model)
    mix_gain = mix_norm.astype(F32)[:, None, :]
    ffn_gain = ffn_norm.astype(F32)[:, None, :]
    w_down = w_down.astype(BF16)
    hy_w_in, hy_w_out, win_w_in, win_w_out = (w.astype(BF16) for w in (hy_w_in, hy_w_out, win_w_in, win_w_out))
    diff_lambda = diff_lambda.astype(F32)
    diff_subln = diff_subln.astype(F32)[:, None, :]
    win_sink = win_sink.astype(F32)

    xg, ssq = _prenorm(xf, mix_gain, 0)
    for layer in range(depth):
        ffn_norm_next = (ffn_gain, layer)
        if layer % 2 == 0:
            e = layer // 2
            gains, flags = _section_gains([
                (a_width, diff_q_norm[e] * scale), (a_width, diff_k_norm[e]), (a_width, None),
                (b_width, dil_q_norm[e] * scale), (b_width, dil_k_norm[e]), (b_width, None)])
            tn = _largest_tile(math.gcd(a_width, b_width), 1024)
            proj = _inproj(xg, ssq, hy_w_in, e, gains, _tile_flags(flags, tn), cos, sin, tn)
            lam_init = 0.8 - 0.6 * math.exp(-0.3 * layer)
            ao = _diff_attn(proj, diff_lambda, diff_subln, e, batch, seq, a_width, lam_init)
            bo = _dilated_attn(proj, batch, seq, 3 * a_width, b_width)
            xf, xg, ssq = _mm2_res(ao, bo, hy_w_out, e, xf, ffn_norm_next)
        else:
            o = layer // 2
            qd, kd = n_q_heads * HEAD_DIM, n_kv_heads * HEAD_DIM
            gains, flags = _section_gains([
                (qd, win_q_norm[o] * scale), (kd, win_k_norm[o]), (kd, None)])
            tn = _largest_tile(math.gcd(qd, kd), 1024)
            proj = _inproj(xg, ssq, win_w_in, o, gains, _tile_flags(flags, tn), cos, sin, tn)
            att = _window_attn(proj, win_sink, o, batch, seq, n_q_heads, n_kv_heads)
            xf, xg, ssq = _mm_res(att, win_w_out, o, xf, ffn_norm_next, 1024, "outproj")
        hidden = _gateup(xg, ssq, w_gate, w_up, layer)
        if layer + 1 < depth:
            xf, xg, ssq = _mm_res(hidden, w_down, layer, xf, (mix_gain, layer + 1), 512, "down")
        else:
            xf = _mm_res(hidden, w_down, layer, xf, None, 512, "down")
    return xf.reshape(batch, seq, d_model)
```

```python
import functools
import math

import jax
import jax.numpy as jnp
from jax import lax
from jax.experimental import pallas as pl
from jax.experimental.pallas import tpu as pltpu

HEAD_DIM = 128
DIL_PATTERNS = ((128, 1), (512, 4), (2048, 16))
WIN_HALF = 128
ROPE_THETA = 10000.0
EPS = 1e-6
NEG_INF = -1e30
LOG2_E = 1.4426950408889634

V7X_LANES = 128
V7X_MXU_COLS = 256
V7X_VMEM_LIMIT_CAP = 56 * 1024 * 1024

F32 = jnp.float32
BF16 = jnp.bfloat16
NT_DIMS = (((1,), (1,)), ((), ()))


def _params(n_grid_dims, vmem_bytes):
    return pltpu.CompilerParams(
        dimension_semantics=("arbitrary",) * n_grid_dims,
        vmem_limit_bytes=int(min(vmem_bytes, V7X_VMEM_LIMIT_CAP)))


def _largest_tile(n, cap, quantum=V7X_LANES):
    best = None
    t = quantum
    while t <= min(n, cap):
        if n % t == 0:
            best = t
        t += quantum
    assert best is not None, (n, cap)
    return best


def _lane_partial_sumsq(x):
    sq = x * x
    return functools.reduce(lambda a, b: a + b,
                            [sq[:, c:c + V7X_LANES] for c in range(0, x.shape[1], V7X_LANES)])


def _inv_rms(ssq_ref, rows, d_model):
    return lax.rsqrt(jnp.sum(ssq_ref[rows, :], axis=-1, keepdims=True) * (1.0 / d_model) + EPS)


ROW_PARTS = 8


def _row_parts(tm):
    part = tm // ROW_PARTS
    return [slice(r0, r0 + part) for r0 in range(0, tm, part)]


def _prenorm_kernel(x_ref, g_ref, xg_ref, ssq_ref):
    x = x_ref[...]
    xg_ref[...] = (x * g_ref[...]).astype(xg_ref.dtype)
    ssq_ref[...] = _lane_partial_sumsq(x)


def _prenorm(x, gains, layer):
    m, d = x.shape
    tm = _largest_tile(m, 256, 8)
    return pl.pallas_call(
        _prenorm_kernel,
        out_shape=[jax.ShapeDtypeStruct((m, d), BF16), jax.ShapeDtypeStruct((m, V7X_LANES), F32)],
        grid=(m // tm,),
        in_specs=[pl.BlockSpec((tm, d), lambda i: (i, 0)),
                  pl.BlockSpec((None, 1, d), lambda i: (layer, 0, 0))],
        out_specs=[pl.BlockSpec((tm, d), lambda i: (i, 0)),
                   pl.BlockSpec((tm, V7X_LANES), lambda i: (i, 0))],
        compiler_params=_params(1, 2 * tm * d * (4 + 2) + 4 * tm * d * 4),
        name="prenorm",
    )(x, gains)


def _emit_residual(res, rest, with_next):
    if not with_next:
        (o_ref,) = rest
        o_ref[...] = res
        return
    g_ref, o_ref, xg_ref, ssq_ref = rest
    o_ref[...] = res
    xg_ref[...] = (res * g_ref[...]).astype(xg_ref.dtype)
    part = _lane_partial_sumsq(res)
    j = pl.program_id(1)

    @pl.when(j == 0)
    def _():
        ssq_ref[...] = part

    @pl.when(j != 0)
    def _():
        ssq_ref[...] += part


def _residual_specs(m, n, tm, tn, next_norm):
    out_shape = [jax.ShapeDtypeStruct((m, n), F32)]
    out_specs = [pl.BlockSpec((tm, tn), lambda i, j: (i, j))]
    if next_norm is None:
        return [], [], out_shape[0], out_specs[0]
    gains, layer = next_norm
    out_shape += [jax.ShapeDtypeStruct((m, n), BF16), jax.ShapeDtypeStruct((m, V7X_LANES), F32)]
    out_specs += [pl.BlockSpec((tm, tn), lambda i, j: (i, j)),
                  pl.BlockSpec((tm, V7X_LANES), lambda i, j: (i, 0))]
    return [pl.BlockSpec((None, 1, tn), lambda i, j: (layer, 0, j))], [gains], out_shape, out_specs


def _norm_rope(x, g, cos, sin):
    ms = jnp.mean(x * x, axis=-1, keepdims=True)
    y = x * lax.rsqrt(ms + EPS) * g
    return y * cos + pltpu.roll(y, HEAD_DIM // 2, axis=1) * sin


def _inproj_kernel(flag_ref, a_ref, ssq_ref, b_ref, g_ref, cos_ref, sin_ref, o_ref):
    j = pl.program_id(1)
    tm, k = a_ref.shape

    @pl.when(flag_ref[j] == 1)
    def _():
        b = b_ref[...]
        accs = [(rows, jnp.dot(a_ref[rows, :], b, preferred_element_type=F32)) for rows in _row_parts(tm)]
        for rows, acc in accs:
            acc = acc * _inv_rms(ssq_ref, rows, k)
            cos = cos_ref[rows, :]
            sin = sin_ref[rows, :]
            for c in range(acc.shape[1] // HEAD_DIM):
                sl = slice(c * HEAD_DIM, (c + 1) * HEAD_DIM)
                o_ref[rows, sl] = _norm_rope(acc[:, sl], g_ref[:, sl], cos, sin).astype(o_ref.dtype)

    @pl.when(flag_ref[j] == 0)
    def _():
        acc = jnp.dot(a_ref[...], b_ref[...], preferred_element_type=F32)
        o_ref[...] = (acc * _inv_rms(ssq_ref, slice(None), k)).astype(o_ref.dtype)


def _inproj(xg, ssq, w, layer, gain_cols, flags, cos, sin, tn):
    m, k = xg.shape
    n = w.shape[2]
    s = cos.shape[0]
    tm = _largest_tile(s, 1024, 8)
    n_pos_blocks = s // tm
    vmem = 2 * (tm * k * 2 + k * tn * 2 + tm * tn * 2 + 2 * tm * HEAD_DIM * 4) + 3 * tm * tn * 4
    return pl.pallas_call(
        _inproj_kernel,
        out_shape=jax.ShapeDtypeStruct((m, n), BF16),
        grid_spec=pltpu.PrefetchScalarGridSpec(
            num_scalar_prefetch=1,
            grid=(m // tm, n // tn),
            in_specs=[pl.BlockSpec((tm, k), lambda i, j, f: (i, 0)),
                      pl.BlockSpec((tm, V7X_LANES), lambda i, j, f: (i, 0)),
                      pl.BlockSpec((None, k, tn), lambda i, j, f: (layer, 0, j)),
                      pl.BlockSpec((1, tn), lambda i, j, f: (0, j)),
                      pl.BlockSpec((tm, HEAD_DIM), lambda i, j, f: (i % n_pos_blocks, 0)),
                      pl.BlockSpec((tm, HEAD_DIM), lambda i, j, f: (i % n_pos_blocks, 0))],
            out_specs=pl.BlockSpec((tm, tn), lambda i, j, f: (i, j))),
        compiler_params=_params(2, vmem),
        name="inproj",
    )(flags, xg, ssq, w, gain_cols, cos, sin)


def _mm_res_kernel(a_ref, b_ref, r_ref, *rest, with_next):
    _emit_residual(r_ref[...] + jnp.dot(a_ref[...], b_ref[...], preferred_element_type=F32), rest, with_next)


def _mm_res(a, w, layer, r, next_norm, tm_cap, name):
    m, k = a.shape
    n = w.shape[2]
    tm = _largest_tile(m, tm_cap, 8)
    tn = _largest_tile(n, 512)
    extra_specs, extra_ops, out_shape, out_specs = _residual_specs(m, n, tm, tn, next_norm)
    vmem = 2 * (tm * k * 2 + k * tn * 2 + 2 * tm * tn * 4 + tm * tn * 2 + tm * V7X_LANES * 4) + 4 * tm * tn * 4
    return pl.pallas_call(
        functools.partial(_mm_res_kernel, with_next=next_norm is not None),
        out_shape=out_shape,
        grid=(m // tm, n // tn),
        in_specs=[pl.BlockSpec((tm, k), lambda i, j: (i, 0)),
                  pl.BlockSpec((None, k, tn), lambda i, j: (layer, 0, j)),
                  pl.BlockSpec((tm, tn), lambda i, j: (i, j))] + extra_specs,
        out_specs=out_specs,
        compiler_params=_params(2, vmem),
        name=name,
    )(a, w, r, *extra_ops)


def _mm2_res_kernel(a1_ref, b1_ref, a2_ref, b2_ref, r_ref, *rest, with_next):
    acc = jnp.dot(a1_ref[...], b1_ref[...], preferred_element_type=F32)
    acc += jnp.dot(a2_ref[...], b2_ref[...], preferred_element_type=F32)
    _emit_residual(r_ref[...] + acc, rest, with_next)


def _mm2_res(a1, a2, w, layer, r, next_norm):
    m, k1 = a1.shape
    k2 = a2.shape[1]
    assert k1 == k2, "the two head groups index w[layer] as row blocks 0 and 1"
    n = w.shape[2]
    tm = _largest_tile(m, 1024, 8)
    tn = _largest_tile(n, 512)
    extra_specs, extra_ops, out_shape, out_specs = _residual_specs(m, n, tm, tn, next_norm)
    vmem = (2 * (tm * (k1 + k2) * 2 + (k1 + k2) * tn * 2 + 2 * tm * tn * 4 + tm * tn * 2 + tm * V7X_LANES * 4)
            + 4 * tm * tn * 4)
    return pl.pallas_call(
        functools.partial(_mm2_res_kernel, with_next=next_norm is not None),
        out_shape=out_shape,
        grid=(m // tm, n // tn),
        in_specs=[pl.BlockSpec((tm, k1), lambda i, j: (i, 0)),
                  pl.BlockSpec((None, k1, tn), lambda i, j: (layer, 0, j)),
                  pl.BlockSpec((tm, k2), lambda i, j: (i, 0)),
                  pl.BlockSpec((None, k2, tn), lambda i, j: (layer, 1, j)),
                  pl.BlockSpec((tm, tn), lambda i, j: (i, j))] + extra_specs,
        out_specs=out_specs,
        compiler_params=_params(2, vmem),
        name="outproj2",
    )(a1, w, a2, w, r, *extra_ops)


def _gateup_kernel(a_ref, ssq_ref, wg_ref, wu_ref, o_ref):
    wg = wg_ref[...].astype(BF16)
    wu = wu_ref[...].astype(BF16)
    parts = []
    for rows in _row_parts(a_ref.shape[0]):
        a = a_ref[rows, :]
        parts.append((rows, jnp.dot(a, wg, preferred_element_type=F32), jnp.dot(a, wu, preferred_element_type=F32)))
    for rows, g, u in parts:
        inv = _inv_rms(ssq_ref, rows, a_ref.shape[1])
        e = jnp.exp2(g * (inv * -LOG2_E))
        o_ref[rows, :] = ((g * u) * ((inv * inv) / (1.0 + e))).astype(o_ref.dtype)


def _gateup(xg, ssq, wg, wu, layer):
    m, k = xg.shape
    n = wg.shape[2]
    tm = _largest_tile(m, 2048, 8)
    tn = _largest_tile(n, 512, V7X_MXU_COLS)
    w_bytes = wg.dtype.itemsize
    vmem = tm * k * 2 + 2 * (2 * k * tn * w_bytes + tm * tn * 2) + 2 * k * tn * 2 + 6 * tm * tn * 4
    return pl.pallas_call(
        _gateup_kernel,
        out_shape=jax.ShapeDtypeStruct((m, n), BF16),
        grid=(m // tm, n // tn),
        in_specs=[pl.BlockSpec((tm, k), lambda i, j: (i, 0), pipeline_mode=pl.Buffered(1)),
                  pl.BlockSpec((tm, V7X_LANES), lambda i, j: (i, 0)),
                  pl.BlockSpec((None, k, tn), lambda i, j: (layer, 0, j)),
                  pl.BlockSpec((None, k, tn), lambda i, j: (layer, 0, j))],
        out_specs=pl.BlockSpec((tm, tn), lambda i, j: (i, j)),
        compiler_params=_params(2, vmem),
        name="gateup",
    )(xg, ssq, wg, wu)


DIFF_ROW_CHUNK = 256


def _diff_attn_kernel(q_ref, k_ref, v_ref, lam_ref, subln_ref, o_ref, *, lam_init):
    q = q_ref[...]
    k = k_ref[...]
    v = v_ref[...]
    lf = lam_ref[...]
    lam = (jnp.exp(jnp.sum(lf[0:1] * lf[1:2], axis=-1, keepdims=True))
           - jnp.exp(jnp.sum(lf[2:3] * lf[3:4], axis=-1, keepdims=True)) + lam_init)

    halves = (slice(0, HEAD_DIM), slice(HEAD_DIM, 2 * HEAD_DIM))
    tq = q.shape[0]
    chunk = math.gcd(tq, DIFF_ROW_CHUNK)
    rows = [slice(r0, r0 + chunk) for r0 in range(0, tq, chunk)]

    def scores(rs):
        return [lax.dot_general(q[rs, c], k[:, c], NT_DIMS, preferred_element_type=F32) for c in halves]

    def finish(ss):
        ps = []
        for s in ss:
            p = jnp.exp2(s - jnp.max(s, axis=-1, keepdims=True))
            ps.append((p, jnp.sum(p, axis=-1, keepdims=True)))
        (p1, l1), (p2, l2) = ps
        a = p1 - p2 * (lam * l1 / l2)
        return jnp.dot(a.astype(BF16), v, preferred_element_type=F32) / l1

    pending = scores(rows[0])
    outs = []
    for nxt in rows[1:]:
        ahead = scores(nxt)
        outs.append(finish(pending))
        pending = ahead
    outs.append(finish(pending))
    o = jnp.concatenate(outs, axis=0)
    ms = jnp.mean(o * o, axis=-1, keepdims=True)
    y = o * lax.rsqrt(ms + EPS) * subln_ref[...]
    o_ref[...] = (y * (1.0 - lam_init)).astype(o_ref.dtype)


def _diff_attn(proj, lambdas, subln, layer, batch, seq, a_width, lam_init):
    m = proj.shape[0]
    hd2 = 2 * HEAD_DIM
    n_heads = a_width // hd2
    tq = _largest_tile(seq, 2048, 8)
    nq = seq // tq
    vmem = 2 * (2 * tq * hd2 * 2 + 2 * seq * hd2 * 2) + 6 * tq * seq * 4
    return pl.pallas_call(
        functools.partial(_diff_attn_kernel, lam_init=lam_init),
        out_shape=jax.ShapeDtypeStruct((m, a_width), BF16),
        grid=(batch, n_heads, nq),
        in_specs=[pl.BlockSpec((tq, hd2), lambda b, h, t: (b * nq + t, h)),
                  pl.BlockSpec((seq, hd2), lambda b, h, t: (b, n_heads + h)),
                  pl.BlockSpec((seq, hd2), lambda b, h, t: (b, 2 * n_heads + h)),
                  pl.BlockSpec((None, 4, HEAD_DIM), lambda b, h, t: (layer, 0, 0)),
                  pl.BlockSpec((None, 1, hd2), lambda b, h, t: (layer, 0, 0))],
        out_specs=pl.BlockSpec((tq, hd2), lambda b, h, t: (b * nq + t, h)),
        compiler_params=_params(3, vmem),
        name="diff_attn",
    )(proj, proj, proj, lambdas, subln)


BAND_TQ = 128


def _band_geometry(n, half_width):
    tq = min(BAND_TQ, n)
    kw = min(n, tq + 2 * half_width)
    return tq, kw, n // tq


def _band_key_start(q0, n, kw, half_width):
    return pl.multiple_of(jnp.clip(q0 - half_width, 0, n - kw), half_width)


def _band_mask(tq, kw, k0_minus_q0, half_width):
    rel = (lax.broadcasted_iota(jnp.int32, (tq, kw), 1) + k0_minus_q0
           - lax.broadcasted_iota(jnp.int32, (tq, kw), 0))
    return jnp.abs(rel) <= half_width


def _band_tiles(tiles, sinks=None):
    sinks = [None] * len(tiles) if sinks is None else sinks
    scores = [lax.dot_general(q, k, NT_DIMS, preferred_element_type=F32) for (q, k, _, _) in tiles]
    probs = []
    for s, (_, _, _, valid), sk in zip(scores, tiles, sinks):
        s = jnp.where(valid, s, NEG_INF)
        m = jnp.max(s, axis=-1, keepdims=True)
        if sk is not None:
            m = jnp.maximum(m, sk)
        p = jnp.exp2(s - m)
        l = jnp.sum(p, axis=-1, keepdims=True)
        if sk is not None:
            l = l + jnp.exp2(sk - m)
        probs.append((p.astype(BF16), m, l))
    return [(jnp.dot(p, v, preferred_element_type=F32) / l, m + jnp.log2(l))
            for (p, m, l), (_, _, v, _) in zip(probs, tiles)]


def _fold(dst_ref, src_ref, factor, n_blocks, cast=None):
    n_tensors, rows = src_ref.shape[0], src_ref.shape[1]
    src_block = rows // n_blocks
    dst_block = src_block // factor
    for i in range(n_tensors):
        for j in range(n_blocks):
            for c in range(factor):
                v = src_ref[i, pl.ds(j * src_block + c, dst_block, stride=factor), :]
                dst_ref[i, (j * factor + c) * dst_block:(j * factor + c + 1) * dst_block, :] = (
                    v if cast is None else v.astype(cast))


def _unfold(dst_ref, src_ref, factor, n_blocks):
    n_tensors, rows = src_ref.shape[0], src_ref.shape[1]
    dst_block = rows // n_blocks
    src_block = dst_block // factor
    for i in range(n_tensors):
        for j in range(n_blocks):
            for c in range(factor):
                dst_ref[i, pl.ds(j * dst_block + c, src_block, stride=factor), :] = (
                    src_ref[i, (j * factor + c) * src_block:(j * factor + c + 1) * src_block, :])


def _dilated_kernel(q_ref, k_ref, v_ref, o_ref, nat_ref, x4_ref, x16_ref, r16_ref, u4_ref, r4_ref):
    seq = q_ref.shape[0]
    (w1, d1), (w4, d4), (w16, d16) = DIL_PATTERNS
    assert (d1, d4, d16) == (1, 4, 16) and w1 // (2 * d1) == w4 // (2 * d4) == w16 // (2 * d16)
    hw = w1 // 2

    for i, ref in enumerate((q_ref, k_ref, v_ref)):
        nat_ref[i] = ref[...].astype(F32)
    _fold(x4_ref, nat_ref, 4, 1)
    _fold(x16_ref, x4_ref, 4, 4, cast=BF16)

    sub16 = seq // 16
    tq, kw, n_tiles = _band_geometry(sub16, hw)
    assert n_tiles == 1 and kw == sub16
    valid16 = _band_mask(tq, kw, 0, hw)
    blocks = [slice(r * sub16, (r + 1) * sub16) for r in range(16)]
    tiles = [(x16_ref[0, rows, :], x16_ref[1, rows, :], x16_ref[2, rows, :], valid16) for rows in blocks]
    for rows, (o, lse) in zip(blocks, _band_tiles(tiles)):
        r16_ref[0, rows, :] = o
        r16_ref[1, rows, :] = jnp.broadcast_to(lse, (sub16, HEAD_DIM))
    _unfold(u4_ref, r16_ref, 4, 4)

    sub4 = seq // 4
    tq, kw, n_tiles = _band_geometry(sub4, hw)
    blocks, tiles = [], []
    for c in range(4):
        for t in range(n_tiles):
            q0 = t * tq
            k0 = min(max(q0 - hw, 0), sub4 - kw)
            qrows = slice(c * sub4 + q0, c * sub4 + q0 + tq)
            krows = slice(c * sub4 + k0, c * sub4 + k0 + kw)
            blocks.append(qrows)
            tiles.append((x4_ref[0, qrows, :].astype(BF16), x4_ref[1, krows, :].astype(BF16),
                          x4_ref[2, krows, :].astype(BF16), _band_mask(tq, kw, k0 - q0, hw)))
    for qrows, (o, lse) in zip(blocks, _band_tiles(tiles)):
        r4_ref[0, qrows, :] = o
        r4_ref[1, qrows, :] = jnp.broadcast_to(lse, (tq, HEAD_DIM))
    _unfold(r16_ref, u4_ref, 4, 1)
    _unfold(nat_ref, r4_ref, 4, 1)

    tq, kw, n_tiles = _band_geometry(seq, hw)
    blocks, tiles = [], []
    for t in range(n_tiles):
        q0 = t * tq
        k0 = min(max(q0 - hw, 0), seq - kw)
        qrows, krows = slice(q0, q0 + tq), slice(k0, k0 + kw)
        blocks.append(qrows)
        tiles.append((q_ref[qrows, :], k_ref[krows, :], v_ref[krows, :], _band_mask(tq, kw, k0 - q0, hw)))
    for qrows, (o, lse) in zip(blocks, _band_tiles(tiles)):
        lses = [lse, nat_ref[1, qrows, :], r16_ref[1, qrows, :]]
        outs = [o, nat_ref[0, qrows, :], r16_ref[0, qrows, :]]
        top = functools.reduce(jnp.maximum, lses)
        ws = [jnp.exp2(x - top) for x in lses]
        num = functools.reduce(lambda a, b: a + b, [w * x for w, x in zip(ws, outs)])
        den = functools.reduce(lambda a, b: a + b, ws)
        o_ref[qrows, :] = (num / den).astype(o_ref.dtype)


def _dilated_attn(proj, batch, seq, col0, b_width):
    m = proj.shape[0]
    n_heads = b_width // HEAD_DIM
    c0 = col0 // HEAD_DIM
    plane = seq * HEAD_DIM
    scratch_bytes = (3 + 3 + 2 + 2 + 2) * plane * 4 + 3 * plane * 2
    vmem = 2 * 4 * plane * 2 + scratch_bytes + 12 * 1024 * 1024
    spec = lambda off: pl.BlockSpec((seq, HEAD_DIM), lambda b, h: (b, c0 + off + h))
    return pl.pallas_call(
        _dilated_kernel,
        out_shape=jax.ShapeDtypeStruct((m, b_width), BF16),
        grid=(batch, n_heads),
        in_specs=[spec(0), spec(n_heads), spec(2 * n_heads)],
        out_specs=pl.BlockSpec((seq, HEAD_DIM), lambda b, h: (b, h)),
        scratch_shapes=[pltpu.VMEM((3, seq, HEAD_DIM), F32), pltpu.VMEM((3, seq, HEAD_DIM), F32),
                        pltpu.VMEM((3, seq, HEAD_DIM), BF16), pltpu.VMEM((2, seq, HEAD_DIM), F32),
                        pltpu.VMEM((2, seq, HEAD_DIM), F32), pltpu.VMEM((2, seq, HEAD_DIM), F32)],
        compiler_params=_params(2, vmem),
        name="dilated_attn",
    )(proj, proj, proj)


WIN_TILES_PER_STEP = 4


def _window_kernel(sink_ref, q_ref, k_ref, v_ref, o_ref, *, group, layer):
    seq = k_ref.shape[0]
    hk = pl.program_id(1)
    tq, kw, n_tiles = _band_geometry(seq, WIN_HALF)
    per_step = math.gcd(n_tiles, WIN_TILES_PER_STEP)

    def tiles_body(step, carry):
        dests, tiles, sinks = [], [], []
        for tt in range(per_step):
            q0 = pl.multiple_of((step * per_step + tt) * tq, tq)
            k0 = _band_key_start(q0, seq, kw, WIN_HALF)
            qrows, krows = pl.ds(q0, tq), pl.ds(k0, kw)
            k = k_ref[krows, :]
            v = v_ref[krows, :]
            valid = _band_mask(tq, kw, k0 - q0, WIN_HALF)
            for g in range(group):
                cols = slice(g * HEAD_DIM, (g + 1) * HEAD_DIM)
                dests.append((qrows, cols))
                tiles.append((q_ref[qrows, cols], k, v, valid))
                sinks.append(sink_ref[layer, hk * group + g] * LOG2_E)
        for (qrows, cols), (o, _) in zip(dests, _band_tiles(tiles, sinks)):
            o_ref[qrows, cols] = o.astype(o_ref.dtype)
        return carry

    lax.fori_loop(0, n_tiles // per_step, tiles_body, 0)


def _window_attn(proj, sinks, layer, batch, seq, n_q_heads, n_kv_heads):
    m = proj.shape[0]
    group = n_q_heads // n_kv_heads
    gw = group * HEAD_DIM
    vmem = 2 * (2 * seq * gw * 2 + 2 * seq * HEAD_DIM * 2) + 16 * 1024 * 1024
    return pl.pallas_call(
        functools.partial(_window_kernel, group=group, layer=layer),
        out_shape=jax.ShapeDtypeStruct((m, n_q_heads * HEAD_DIM), BF16),
        grid_spec=pltpu.PrefetchScalarGridSpec(
            num_scalar_prefetch=1,
            grid=(batch, n_kv_heads),
            in_specs=[pl.BlockSpec((seq, gw), lambda b, h, s: (b, h)),
                      pl.BlockSpec((seq, HEAD_DIM), lambda b, h, s: (b, n_q_heads + h)),
                      pl.BlockSpec((seq, HEAD_DIM), lambda b, h, s: (b, n_q_heads + n_kv_heads + h))],
            out_specs=pl.BlockSpec((seq, gw), lambda b, h, s: (b, h))),
        compiler_params=_params(2, vmem),
        name="window_attn",
    )(sinks, proj, proj, proj)


def _rope_tables(seq):
    inv_freq = ROPE_THETA ** (-jnp.arange(0, HEAD_DIM, 2, dtype=F32) / HEAD_DIM)
    ang = jnp.arange(seq, dtype=F32)[:, None] * inv_freq[None, :]
    ang = jnp.concatenate([ang, ang], axis=-1)
    sign = jnp.where(jnp.arange(HEAD_DIM) < HEAD_DIM // 2, -1.0, 1.0).astype(F32)
    return jnp.cos(ang), jnp.sin(ang) * sign


def _section_gains(sections):
    cols, flags = [], []
    for width, g in sections:
        reps = width // HEAD_DIM
        if g is None:
            cols.append(jnp.ones((width,), F32))
            flags += [0] * reps
        else:
            cols.append(jnp.tile(g.astype(F32), reps))
            flags += [1] * reps
    return jnp.concatenate(cols).reshape(1, -1), flags


def _tile_flags(flags, tn):
    per = tn // HEAD_DIM
    tiles = [flags[i:i + per] for i in range(0, len(flags), per)]
    assert all(len(set(t)) == 1 for t in tiles), "a column tile mixes normed and plain heads"
    return jnp.asarray([t[0] for t in tiles], jnp.int32)


def kernel(x, mix_norm, ffn_norm, w_gate, w_up, w_down, hy_w_in, hy_w_out, diff_q_norm, diff_k_norm,
           diff_lambda, diff_subln, dil_q_norm, dil_k_norm, win_w_in, win_w_out, win_q_norm, win_k_norm,
           win_sink):
    batch, seq, d_model = x.shape
    depth = mix_norm.shape[0]
    a_width = d_model // 2
    b_width = d_model - a_width
    n_q_heads = d_model // HEAD_DIM
    n_kv_heads = (win_w_in.shape[-1] // HEAD_DIM - n_q_heads) // 2
    scale = HEAD_DIM ** -0.5 * LOG2_E

    cos, sin = _rope_tables(seq)
    xf = x.reshape(batch * seq, d_model)
    mix_gain = mix_norm.astype(F32)[:, None, :]
    ffn_gain = ffn_norm.astype(F32)[:, None, :]
    w_down = w_down.astype(BF16)
    hy_w_in, hy_w_out, win_w_in, win_w_out = (w.astype(BF16) for w in (hy_w_in, hy_w_out, win_w_in, win_w_out))
    diff_lambda = diff_lambda.astype(F32)
    diff_subln = diff_subln.astype(F32)[:, None, :]
    win_sink = win_sink.astype(F32)

    xg, ssq = _prenorm(xf, mix_gain, 0)
    for layer in range(depth):
        ffn_norm_next = (ffn_gain, layer)
        if layer % 2 == 0:
            e = layer // 2
            gains, flags = _section_gains([
                (a_width, diff_q_norm[e] * scale), (a_width, diff_k_norm[e]), (a_width, None),
                (b_width, dil_q_norm[e] * scale), (b_width, dil_k_norm[e]), (b_width, None)])
            tn = _largest_tile(math.gcd(a_width, b_width), 1024)
            proj = _inproj(xg, ssq, hy_w_in, e, gains, _tile_flags(flags, tn), cos, sin, tn)
            lam_init = 0.8 - 0.6 * math.exp(-0.3 * layer)
            ao = _diff_attn(proj, diff_lambda, diff_subln, e, batch, seq, a_width, lam_init)
            bo = _dilated_attn(proj, batch, seq, 3 * a_width, b_width)
            xf, xg, ssq = _mm2_res(ao, bo, hy_w_out, e, xf, ffn_norm_next)
        else:
            o = layer // 2
            qd, kd = n_q_heads * HEAD_DIM, n_kv_heads * HEAD_DIM
            gains, flags = _section_gains([
                (qd, win_q_norm[o] * scale), (kd, win_k_norm[o]), (kd, None)])
            tn = _largest_tile(math.gcd(qd, kd), 1024)
            proj = _inproj(xg, ssq, win_w_in, o, gains, _tile_flags(flags, tn), cos, sin, tn)
            att = _window_attn(proj, win_sink, o, batch, seq, n_q_heads, n_kv_heads)
            xf, xg, ssq = _mm_res(att, win_w_out, o, xf, ffn_norm_next, 1024, "outproj")
        hidden = _gateup(xg, ssq, w_gate, w_up, layer)
        if layer + 1 < depth:
            xf, xg, ssq = _mm_res(hidden, w_down, layer, xf, (mix_gain, layer + 1), 512, "down")
        else:
            xf = _mm_res(hidden, w_down, layer, xf, None, 512, "down")
    return xf.reshape(batch, seq, d_model)
```

```python
import functools
import math

import jax
import jax.numpy as jnp
from jax import lax
from jax.experimental import pallas as pl
from jax.experimental.pallas import tpu as pltpu

HEAD_DIM = 128
DIL_PATTERNS = ((128, 1), (512, 4), (2048, 16))
WIN_HALF = 128
ROPE_THETA = 10000.0
EPS = 1e-6
NEG_INF = -1e30
LOG2_E = 1.4426950408889634

V7X_LANES = 128
V7X_MXU_COLS = 256
V7X_VMEM_LIMIT_CAP = 58 * 1024 * 1024

F32 = jnp.float32
BF16 = jnp.bfloat16
NT_DIMS = (((1,), (1,)), ((), ()))


def _params(n_grid_dims, vmem_bytes):
    return pltpu.CompilerParams(
        dimension_semantics=("arbitrary",) * n_grid_dims,
        vmem_limit_bytes=int(min(vmem_bytes, V7X_VMEM_LIMIT_CAP)))


def _largest_tile(n, cap, quantum=V7X_LANES):
    best = None
    t = quantum
    while t <= min(n, cap):
        if n % t == 0:
            best = t
        t += quantum
    assert best is not None, (n, cap)
    return best


def _lane_partial_sumsq(x):
    sq = x * x
    return functools.reduce(lambda a, b: a + b,
                            [sq[:, c:c + V7X_LANES] for c in range(0, x.shape[1], V7X_LANES)])


def _inv_rms(ssq_ref, rows, d_model):
    return lax.rsqrt(jnp.sum(ssq_ref[rows, :], axis=-1, keepdims=True) * (1.0 / d_model) + EPS)


ROW_PARTS = 8


def _row_parts(tm):
    part = tm // ROW_PARTS
    return [slice(r0, r0 + part) for r0 in range(0, tm, part)]


def _prenorm_kernel(x_ref, g_ref, xg_ref, ssq_ref):
    x = x_ref[...]
    xg_ref[...] = (x * g_ref[...]).astype(xg_ref.dtype)
    ssq_ref[...] = _lane_partial_sumsq(x)


def _prenorm(x, gains, layer):
    m, d = x.shape
    tm = _largest_tile(m, 256, 8)
    return pl.pallas_call(
        _prenorm_kernel,
        out_shape=[jax.ShapeDtypeStruct((m, d), BF16), jax.ShapeDtypeStruct((m, V7X_LANES), F32)],
        grid=(m // tm,),
        in_specs=[pl.BlockSpec((tm, d), lambda i: (i, 0)),
                  pl.BlockSpec((None, 1, d), lambda i: (layer, 0, 0))],
        out_specs=[pl.BlockSpec((tm, d), lambda i: (i, 0)),
                   pl.BlockSpec((tm, V7X_LANES), lambda i: (i, 0))],
        compiler_params=_params(1, 2 * tm * d * (4 + 2) + 4 * tm * d * 4),
        name="prenorm",
    )(x, gains)


def _emit_residual(res, rest, with_next):
    if not with_next:
        (o_ref,) = rest
        o_ref[...] = res
        return
    g_ref, o_ref, xg_ref, ssq_ref = rest
    o_ref[...] = res
    xg_ref[...] = (res * g_ref[...]).astype(xg_ref.dtype)
    part = _lane_partial_sumsq(res)
    j = pl.program_id(1)

    @pl.when(j == 0)
    def _():
        ssq_ref[...] = part

    @pl.when(j != 0)
    def _():
        ssq_ref[...] += part


def _residual_specs(m, n, tm, tn, next_norm):
    out_shape = [jax.ShapeDtypeStruct((m, n), F32)]
    out_specs = [pl.BlockSpec((tm, tn), lambda i, j: (i, j))]
    if next_norm is None:
        return [], [], out_shape[0], out_specs[0]
    gains, layer = next_norm
    out_shape += [jax.ShapeDtypeStruct((m, n), BF16), jax.ShapeDtypeStruct((m, V7X_LANES), F32)]
    out_specs += [pl.BlockSpec((tm, tn), lambda i, j: (i, j)),
                  pl.BlockSpec((tm, V7X_LANES), lambda i, j: (i, 0))]
    return [pl.BlockSpec((None, 1, tn), lambda i, j: (layer, 0, j))], [gains], out_shape, out_specs


def _norm_rope(x, g, cos, sin):
    ms = jnp.mean(x * x, axis=-1, keepdims=True)
    y = x * lax.rsqrt(ms + EPS) * g
    return y * cos + pltpu.roll(y, HEAD_DIM // 2, axis=1) * sin


def _inproj_kernel(flag_ref, a_ref, ssq_ref, b_ref, g_ref, cos_ref, sin_ref, o_ref):
    j = pl.program_id(1)
    tm, k = a_ref.shape

    @pl.when(flag_ref[j] == 1)
    def _():
        b = b_ref[...]
        accs = [(rows, jnp.dot(a_ref[rows, :], b, preferred_element_type=F32)) for rows in _row_parts(tm)]
        for rows, acc in accs:
            acc = acc * _inv_rms(ssq_ref, rows, k)
            cos = cos_ref[rows, :]
            sin = sin_ref[rows, :]
            for c in range(acc.shape[1] // HEAD_DIM):
                sl = slice(c * HEAD_DIM, (c + 1) * HEAD_DIM)
                o_ref[rows, sl] = _norm_rope(acc[:, sl], g_ref[:, sl], cos, sin).astype(o_ref.dtype)

    @pl.when(flag_ref[j] == 0)
    def _():
        acc = jnp.dot(a_ref[...], b_ref[...], preferred_element_type=F32)
        o_ref[...] = (acc * _inv_rms(ssq_ref, slice(None), k)).astype(o_ref.dtype)


def _inproj(xg, ssq, w, layer, gain_cols, flags, cos, sin, tn):
    m, k = xg.shape
    n = w.shape[2]
    s = cos.shape[0]
    tm = _largest_tile(s, 1024, 8)
    n_pos_blocks = s // tm
    vmem = 2 * (tm * k * 2 + k * tn * 2 + tm * tn * 2 + 2 * tm * HEAD_DIM * 4) + 3 * tm * tn * 4
    return pl.pallas_call(
        _inproj_kernel,
        out_shape=jax.ShapeDtypeStruct((m, n), BF16),
        grid_spec=pltpu.PrefetchScalarGridSpec(
            num_scalar_prefetch=1,
            grid=(m // tm, n // tn),
            in_specs=[pl.BlockSpec((tm, k), lambda i, j, f: (i, 0)),
                      pl.BlockSpec((tm, V7X_LANES), lambda i, j, f: (i, 0)),
                      pl.BlockSpec((None, k, tn), lambda i, j, f: (layer, 0, j)),
                      pl.BlockSpec((1, tn), lambda i, j, f: (0, j)),
                      pl.BlockSpec((tm, HEAD_DIM), lambda i, j, f: (i % n_pos_blocks, 0)),
                      pl.BlockSpec((tm, HEAD_DIM), lambda i, j, f: (i % n_pos_blocks, 0))],
            out_specs=pl.BlockSpec((tm, tn), lambda i, j, f: (i, j))),
        compiler_params=_params(2, vmem),
        name="inproj",
    )(flags, xg, ssq, w, gain_cols, cos, sin)


def _mm_res_kernel(a_ref, b_ref, r_ref, *rest, with_next):
    _emit_residual(r_ref[...] + jnp.dot(a_ref[...], b_ref[...], preferred_element_type=F32), rest, with_next)


def _mm_res(a, w, layer, r, next_norm, tm_cap, name):
    m, k = a.shape
    n = w.shape[2]
    tm = _largest_tile(m, tm_cap, 8)
    tn = _largest_tile(n, 512)
    extra_specs, extra_ops, out_shape, out_specs = _residual_specs(m, n, tm, tn, next_norm)
    vmem = 2 * (tm * k * 2 + k * tn * 2 + 2 * tm * tn * 4 + tm * tn * 2 + tm * V7X_LANES * 4) + 4 * tm * tn * 4
    return pl.pallas_call(
        functools.partial(_mm_res_kernel, with_next=next_norm is not None),
        out_shape=out_shape,
        grid=(m // tm, n // tn),
        in_specs=[pl.BlockSpec((tm, k), lambda i, j: (i, 0)),
                  pl.BlockSpec((None, k, tn), lambda i, j: (layer, 0, j)),
                  pl.BlockSpec((tm, tn), lambda i, j: (i, j))] + extra_specs,
        out_specs=out_specs,
        compiler_params=_params(2, vmem),
        name=name,
    )(a, w, r, *extra_ops)


def _mm2_res_kernel(a1_ref, b1_ref, a2_ref, b2_ref, r_ref, *rest, with_next):
    acc = jnp.dot(a1_ref[...], b1_ref[...], preferred_element_type=F32)
    acc += jnp.dot(a2_ref[...], b2_ref[...], preferred_element_type=F32)
    _emit_residual(r_ref[...] + acc, rest, with_next)


def _mm2_res(a1, a2, w, layer, r, next_norm):
    m, k1 = a1.shape
    k2 = a2.shape[1]
    assert k1 == k2, "the two head groups index w[layer] as row blocks 0 and 1"
    n = w.shape[2]
    tm = _largest_tile(m, 1024, 8)
    tn = _largest_tile(n, 512)
    extra_specs, extra_ops, out_shape, out_specs = _residual_specs(m, n, tm, tn, next_norm)
    vmem = (2 * (tm * (k1 + k2) * 2 + (k1 + k2) * tn * 2 + 2 * tm * tn * 4 + tm * tn * 2 + tm * V7X_LANES * 4)
            + 4 * tm * tn * 4)
    return pl.pallas_call(
        functools.partial(_mm2_res_kernel, with_next=next_norm is not None),
        out_shape=out_shape,
        grid=(m // tm, n // tn),
        in_specs=[pl.BlockSpec((tm, k1), lambda i, j: (i, 0)),
                  pl.BlockSpec((None, k1, tn), lambda i, j: (layer, 0, j)),
                  pl.BlockSpec((tm, k2), lambda i, j: (i, 0)),
                  pl.BlockSpec((None, k2, tn), lambda i, j: (layer, 1, j)),
                  pl.BlockSpec((tm, tn), lambda i, j: (i, j))] + extra_specs,
        out_specs=out_specs,
        compiler_params=_params(2, vmem),
        name="outproj2",
    )(a1, w, a2, w, r, *extra_ops)


def _gateup_kernel(a_ref, ssq_ref, wg_ref, wu_ref, o_ref):
    wg = wg_ref[...].astype(BF16)
    wu = wu_ref[...].astype(BF16)
    parts = []
    for rows in _row_parts(a_ref.shape[0]):
        a = a_ref[rows, :]
        parts.append((rows, jnp.dot(a, wg, preferred_element_type=F32), jnp.dot(a, wu, preferred_element_type=F32)))
    for rows, g, u in parts:
        inv = _inv_rms(ssq_ref, rows, a_ref.shape[1])
        e = jnp.exp2(g * (inv * -LOG2_E))
        o_ref[rows, :] = ((g * u) * ((inv * inv) / (1.0 + e))).astype(o_ref.dtype)


def _gateup(xg, ssq, wg, wu, layer):
    m, k = xg.shape
    n = wg.shape[2]
    tm = _largest_tile(m, 2048, 8)
    tn = _largest_tile(n, 512, V7X_MXU_COLS)
    w_bytes = wg.dtype.itemsize
    vmem = 2 * (tm * k * 2 + 2 * k * tn * w_bytes + tm * tn * 2) + 2 * k * tn * 2 + 6 * tm * tn * 4
    return pl.pallas_call(
        _gateup_kernel,
        out_shape=jax.ShapeDtypeStruct((m, n), BF16),
        grid=(m // tm, n // tn),
        in_specs=[pl.BlockSpec((tm, k), lambda i, j: (i, 0)),
                  pl.BlockSpec((tm, V7X_LANES), lambda i, j: (i, 0)),
                  pl.BlockSpec((None, k, tn), lambda i, j: (layer, 0, j)),
                  pl.BlockSpec((None, k, tn), lambda i, j: (layer, 0, j))],
        out_specs=pl.BlockSpec((tm, tn), lambda i, j: (i, j)),
        compiler_params=_params(2, vmem),
        name="gateup",
    )(xg, ssq, wg, wu)


DIFF_ROW_CHUNK = 256


def _diff_attn_kernel(q_ref, k_ref, v_ref, lam_ref, subln_ref, o_ref, *, lam_init):
    q = q_ref[...]
    k = k_ref[...]
    v = v_ref[...]
    lf = lam_ref[...]
    lam = (jnp.exp(jnp.sum(lf[0:1] * lf[1:2], axis=-1, keepdims=True))
           - jnp.exp(jnp.sum(lf[2:3] * lf[3:4], axis=-1, keepdims=True)) + lam_init)

    halves = (slice(0, HEAD_DIM), slice(HEAD_DIM, 2 * HEAD_DIM))
    tq = q.shape[0]
    chunk = math.gcd(tq, DIFF_ROW_CHUNK)
    rows = [slice(r0, r0 + chunk) for r0 in range(0, tq, chunk)]

    def scores(rs):
        return [lax.dot_general(q[rs, c], k[:, c], NT_DIMS, preferred_element_type=F32) for c in halves]

    def finish(ss):
        ps = []
        for s in ss:
            p = jnp.exp2(s - jnp.max(s, axis=-1, keepdims=True))
            ps.append((p, jnp.sum(p, axis=-1, keepdims=True)))
        (p1, l1), (p2, l2) = ps
        a = p1 - p2 * (lam * l1 / l2)
        return jnp.dot(a.astype(BF16), v, preferred_element_type=F32) / l1

    pending = scores(rows[0])
    outs = []
    for nxt in rows[1:]:
        ahead = scores(nxt)
        outs.append(finish(pending))
        pending = ahead
    outs.append(finish(pending))
    o = jnp.concatenate(outs, axis=0)
    ms = jnp.mean(o * o, axis=-1, keepdims=True)
    y = o * lax.rsqrt(ms + EPS) * subln_ref[...]
    o_ref[...] = (y * (1.0 - lam_init)).astype(o_ref.dtype)


def _diff_attn(proj, lambdas, subln, layer, batch, seq, a_width, lam_init):
    m = proj.shape[0]
    hd2 = 2 * HEAD_DIM
    n_heads = a_width // hd2
    tq = _largest_tile(seq, 2048, 8)
    nq = seq // tq
    vmem = 2 * (2 * tq * hd2 * 2 + 2 * seq * hd2 * 2) + 6 * tq * seq * 4
    return pl.pallas_call(
        functools.partial(_diff_attn_kernel, lam_init=lam_init),
        out_shape=jax.ShapeDtypeStruct((m, a_width), BF16),
        grid=(batch, n_heads, nq),
        in_specs=[pl.BlockSpec((tq, hd2), lambda b, h, t: (b * nq + t, h)),
                  pl.BlockSpec((seq, hd2), lambda b, h, t: (b, n_heads + h)),
                  pl.BlockSpec((seq, hd2), lambda b, h, t: (b, 2 * n_heads + h)),
                  pl.BlockSpec((None, 4, HEAD_DIM), lambda b, h, t: (layer, 0, 0)),
                  pl.BlockSpec((None, 1, hd2), lambda b, h, t: (layer, 0, 0))],
        out_specs=pl.BlockSpec((tq, hd2), lambda b, h, t: (b * nq + t, h)),
        compiler_params=_params(3, vmem),
        name="diff_attn",
    )(proj, proj, proj, lambdas, subln)


BAND_TQ = 128


def _band_geometry(n, half_width):
    tq = min(BAND_TQ, n)
    kw = min(n, tq + 2 * half_width)
    return tq, kw, n // tq


def _band_mask(tq, kw, k0_minus_q0, half_width):
    rel = (lax.broadcasted_iota(jnp.int32, (tq, kw), 1) + k0_minus_q0
           - lax.broadcasted_iota(jnp.int32, (tq, kw), 0))
    return jnp.abs(rel) <= half_width


def _band_tiles(tiles, sinks=None):
    sinks = [None] * len(tiles) if sinks is None else sinks
    scores = [lax.dot_general(q, k, NT_DIMS, preferred_element_type=F32) for (q, k, _, _) in tiles]
    probs = []
    for s, (_, _, _, valid), sk in zip(scores, tiles, sinks):
        if isinstance(valid, (list, tuple)):
            width = s.shape[1] // len(valid)
            s = jnp.concatenate([blk if msk is None else jnp.where(msk, blk, NEG_INF) for blk, msk in
                                 ((s[:, c * width:(c + 1) * width], msk) for c, msk in enumerate(valid))], axis=1)
        else:
            s = jnp.where(valid, s, NEG_INF)
        m = jnp.max(s, axis=-1, keepdims=True)
        if sk is not None:
            m = jnp.maximum(m, sk)
        p = jnp.exp2(s - m)
        l = jnp.sum(p, axis=-1, keepdims=True)
        if sk is not None:
            l = l + jnp.exp2(sk - m)
        probs.append((p.astype(BF16), m, l))
    return [(jnp.dot(p, v, preferred_element_type=F32) / l, m + jnp.log2(l))
            for (p, m, l), (_, _, v, _) in zip(probs, tiles)]


DIL_TILES_PER_GROUP = 8


def _fold(dst_ref, src_ref, factor, n_blocks, cast=None):
    n_tensors, rows = src_ref.shape[0], src_ref.shape[1]
    src_block = rows // n_blocks
    dst_block = src_block // factor
    for i in range(n_tensors):
        for j in range(n_blocks):
            for c in range(factor):
                v = src_ref[i, pl.ds(j * src_block + c, dst_block, stride=factor), :]
                dst_ref[i, (j * factor + c) * dst_block:(j * factor + c + 1) * dst_block, :] = (
                    v if cast is None else v.astype(cast))


def _unfold(dst_ref, src_ref, factor, n_blocks):
    n_tensors, rows = src_ref.shape[0], src_ref.shape[1]
    dst_block = rows // n_blocks
    src_block = dst_block // factor
    for i in range(n_tensors):
        for j in range(n_blocks):
            for c in range(factor):
                dst_ref[i, pl.ds(j * dst_block + c, src_block, stride=factor), :] = (
                    src_ref[i, (j * factor + c) * src_block:(j * factor + c + 1) * src_block, :])


def _dilated_kernel(q_ref, k_ref, v_ref, o_ref, nat_ref, x4_ref, x16_ref, r16_ref, u4_ref, r4_ref):
    seq = q_ref.shape[0]
    (w1, d1), (w4, d4), (w16, d16) = DIL_PATTERNS
    assert (d1, d4, d16) == (1, 4, 16) and w1 // (2 * d1) == w4 // (2 * d4) == w16 // (2 * d16)
    hw = w1 // 2

    for i, ref in enumerate((q_ref, k_ref, v_ref)):
        nat_ref[i] = ref[...].astype(F32)
    _fold(x4_ref, nat_ref, 4, 1)
    _fold(x16_ref, x4_ref, 4, 4, cast=BF16)

    sub16 = seq // 16
    tq, kw, n_tiles = _band_geometry(sub16, hw)
    assert n_tiles == 1 and kw == sub16
    valid16 = _band_mask(tq, kw, 0, hw)
    blocks = [slice(r * sub16, (r + 1) * sub16) for r in range(16)]
    tiles = [(x16_ref[0, rows, :], x16_ref[1, rows, :], x16_ref[2, rows, :], valid16) for rows in blocks]
    for rows, (o, lse) in zip(blocks, _band_tiles(tiles)):
        r16_ref[0, rows, :] = o
        r16_ref[1, rows, :] = jnp.broadcast_to(lse, (sub16, HEAD_DIM))
    _unfold(u4_ref, r16_ref, 4, 4)

    sub4 = seq // 4
    tq, kw, n_tiles = _band_geometry(sub4, hw)
    blocks, tiles = [], []
    for c in range(4):
        for t in range(n_tiles):
            q0 = t * tq
            k0 = min(max(q0 - hw, 0), sub4 - kw)
            qrows = slice(c * sub4 + q0, c * sub4 + q0 + tq)
            krows = slice(c * sub4 + k0, c * sub4 + k0 + kw)
            blocks.append(qrows)
            tiles.append((x4_ref[0, qrows, :].astype(BF16), x4_ref[1, krows, :].astype(BF16),
                          x4_ref[2, krows, :].astype(BF16), _band_mask(tq, kw, k0 - q0, hw)))
    for g0 in range(0, len(tiles), DIL_TILES_PER_GROUP):
        group = slice(g0, g0 + DIL_TILES_PER_GROUP)
        for qrows, (o, lse) in zip(blocks[group], _band_tiles(tiles[group])):
            r4_ref[0, qrows, :] = o
            r4_ref[1, qrows, :] = jnp.broadcast_to(lse, (tq, HEAD_DIM))
    _unfold(r16_ref, u4_ref, 4, 1)
    _unfold(nat_ref, r4_ref, 4, 1)

    tq, kw, n_tiles = _band_geometry(seq, hw)
    blocks, tiles = [], []
    for t in range(n_tiles):
        q0 = t * tq
        k0 = min(max(q0 - hw, 0), seq - kw)
        qrows, krows = slice(q0, q0 + tq), slice(k0, k0 + kw)
        blocks.append(qrows)
        tiles.append((q_ref[qrows, :], k_ref[krows, :], v_ref[krows, :], _band_mask(tq, kw, k0 - q0, hw)))
    for g0 in range(0, len(tiles), DIL_TILES_PER_GROUP):
        group = slice(g0, g0 + DIL_TILES_PER_GROUP)
        for qrows, (o, lse) in zip(blocks[group], _band_tiles(tiles[group])):
            lses = [lse, nat_ref[1, qrows, :], r16_ref[1, qrows, :]]
            outs = [o, nat_ref[0, qrows, :], r16_ref[0, qrows, :]]
            top = functools.reduce(jnp.maximum, lses)
            ws = [jnp.exp2(x - top) for x in lses]
            num = functools.reduce(lambda a, b: a + b, [w * x for w, x in zip(ws, outs)])
            den = functools.reduce(lambda a, b: a + b, ws)
            o_ref[qrows, :] = (num / den).astype(o_ref.dtype)


def _dilated_attn(proj, batch, seq, col0, b_width):
    m = proj.shape[0]
    n_heads = b_width // HEAD_DIM
    c0 = col0 // HEAD_DIM
    plane = seq * HEAD_DIM
    scratch_bytes = (3 + 3 + 2 + 2 + 2) * plane * 4 + 3 * plane * 2
    vmem = 2 * 4 * plane * 2 + scratch_bytes + 12 * 1024 * 1024
    spec = lambda off: pl.BlockSpec((seq, HEAD_DIM), lambda b, h: (b, c0 + off + h))
    return pl.pallas_call(
        _dilated_kernel,
        out_shape=jax.ShapeDtypeStruct((m, b_width), BF16),
        grid=(batch, n_heads),
        in_specs=[spec(0), spec(n_heads), spec(2 * n_heads)],
        out_specs=pl.BlockSpec((seq, HEAD_DIM), lambda b, h: (b, h)),
        scratch_shapes=[pltpu.VMEM((3, seq, HEAD_DIM), F32), pltpu.VMEM((3, seq, HEAD_DIM), F32),
                        pltpu.VMEM((3, seq, HEAD_DIM), BF16), pltpu.VMEM((2, seq, HEAD_DIM), F32),
                        pltpu.VMEM((2, seq, HEAD_DIM), F32), pltpu.VMEM((2, seq, HEAD_DIM), F32)],
        compiler_params=_params(2, vmem),
        name="dilated_attn",
    )(proj, proj, proj)


WIN_TILES_PER_STEP = 7


def _window_kernel(sink_ref, q_ref, k_ref, v_ref, o_ref, *, group, layer):
    seq = k_ref.shape[0]
    hk = pl.program_id(1)
    tq, kw, n_tiles = _band_geometry(seq, WIN_HALF)
    assert tq == WIN_HALF and kw == 3 * tq and (n_tiles - 2) % WIN_TILES_PER_STEP == 0, (seq, tq, kw)
    sinks = [sink_ref[layer, hk * group + g] * LOG2_E for g in range(group)]
    heads = [slice(g * HEAD_DIM, (g + 1) * HEAD_DIM) for g in range(group)]

    def run(tile_specs):
        dests, tiles = [], []
        for q0, k0, valid in tile_specs:
            qrows, krows = pl.ds(q0, tq), pl.ds(k0, kw)
            k = k_ref[krows, :]
            v = v_ref[krows, :]
            for cols in heads:
                dests.append((qrows, cols))
                tiles.append((q_ref[qrows, cols], k, v, valid))
        for (qrows, cols), (o, _) in zip(dests, _band_tiles(tiles, sinks * len(tile_specs))):
            o_ref[qrows, cols] = o.astype(o_ref.dtype)

    row = lax.broadcasted_iota(jnp.int32, (tq, tq), 0)
    col = lax.broadcasted_iota(jnp.int32, (tq, tq), 1)
    interior_valid = [col >= row, None, col <= row]

    def interior_body(step, carry):
        specs = []
        for tt in range(WIN_TILES_PER_STEP):
            q0 = pl.multiple_of((1 + step * WIN_TILES_PER_STEP + tt) * tq, tq)
            specs.append((q0, pl.multiple_of(q0 - tq, tq), interior_valid))
        run(specs)
        return carry

    lax.fori_loop(0, (n_tiles - 2) // WIN_TILES_PER_STEP, interior_body, 0)
    last = (n_tiles - 1) * tq
    run([(0, 0, _band_mask(tq, kw, 0, WIN_HALF)), (last, seq - kw, _band_mask(tq, kw, seq - kw - last, WIN_HALF))])


def _window_attn(proj, sinks, layer, batch, seq, n_q_heads, n_kv_heads):
    m = proj.shape[0]
    group = n_q_heads // n_kv_heads
    gw = group * HEAD_DIM
    vmem = 2 * (2 * seq * gw * 2 + 2 * seq * HEAD_DIM * 2) + 16 * 1024 * 1024
    return pl.pallas_call(
        functools.partial(_window_kernel, group=group, layer=layer),
        out_shape=jax.ShapeDtypeStruct((m, n_q_heads * HEAD_DIM), BF16),
        grid_spec=pltpu.PrefetchScalarGridSpec(
            num_scalar_prefetch=1,
            grid=(batch, n_kv_heads),
            in_specs=[pl.BlockSpec((seq, gw), lambda b, h, s: (b, h)),
                      pl.BlockSpec((seq, HEAD_DIM), lambda b, h, s: (b, n_q_heads + h)),
                      pl.BlockSpec((seq, HEAD_DIM), lambda b, h, s: (b, n_q_heads + n_kv_heads + h))],
            out_specs=pl.BlockSpec((seq, gw), lambda b, h, s: (b, h))),
        compiler_params=_params(2, vmem),
        name="window_attn",
    )(sinks, proj, proj, proj)


def _rope_tables(seq):
    inv_freq = ROPE_THETA ** (-jnp.arange(0, HEAD_DIM, 2, dtype=F32) / HEAD_DIM)
    ang = jnp.arange(seq, dtype=F32)[:, None] * inv_freq[None, :]
    ang = jnp.concatenate([ang, ang], axis=-1)
    sign = jnp.where(jnp.arange(HEAD_DIM) < HEAD_DIM // 2, -1.0, 1.0).astype(F32)
    return jnp.cos(ang), jnp.sin(ang) * sign


def _section_gains(sections):
    cols, flags = [], []
    for width, g in sections:
        reps = width // HEAD_DIM
        if g is None:
            cols.append(jnp.ones((width,), F32))
            flags += [0] * reps
        else:
            cols.append(jnp.tile(g.astype(F32), reps))
            flags += [1] * reps
    return jnp.concatenate(cols).reshape(1, -1), flags


def _tile_flags(flags, tn):
    per = tn // HEAD_DIM
    tiles = [flags[i:i + per] for i in range(0, len(flags), per)]
    assert all(len(set(t)) == 1 for t in tiles), "a column tile mixes normed and plain heads"
    return jnp.asarray([t[0] for t in tiles], jnp.int32)


def kernel(x, mix_norm, ffn_norm, w_gate, w_up, w_down, hy_w_in, hy_w_out, diff_q_norm, diff_k_norm,
           diff_lambda, diff_subln, dil_q_norm, dil_k_norm, win_w_in, win_w_out, win_q_norm, win_k_norm,
           win_sink):
    batch, seq, d_model = x.shape
    depth = mix_norm.shape[0]
    a_width = d_model // 2
    b_width = d_model - a_width
    n_q_heads = d_model // HEAD_DIM
    n_kv_heads = (win_w_in.shape[-1] // HEAD_DIM - n_q_heads) // 2
    scale = HEAD_DIM ** -0.5 * LOG2_E

    cos, sin = _rope_tables(seq)
    xf = x.reshape(batch * seq, d_model)
    mix_gain = mix_norm.astype(F32)[:, None, :]
    ffn_gain = ffn_norm.astype(F32)[:, None, :]
    w_down = w_down.astype(BF16)
    hy_w_in, hy_w_out, win_w_in, win_w_out = (w.astype(BF16) for w in (hy_w_in, hy_w_out, win_w_in, win_w_out))
    diff_lambda = diff_lambda.astype(F32)
    diff_subln = diff_subln.astype(F32)[:, None, :]
    win_sink = win_sink.astype(F32)

    xg, ssq = _prenorm(xf, mix_gain, 0)
    for layer in range(depth):
        ffn_norm_next = (ffn_gain, layer)
        if layer % 2 == 0:
            e = layer // 2
            gains, flags = _section_gains([
                (a_width, diff_q_norm[e] * scale), (a_width, diff_k_norm[e]), (a_width, None),
                (b_width, dil_q_norm[e] * scale), (b_width, dil_k_norm[e]), (b_width, None)])
            tn = _largest_tile(math.gcd(a_width, b_width), 1024)
            proj = _inproj(xg, ssq, hy_w_in, e, gains, _tile_flags(flags, tn), cos, sin, tn)
            lam_init = 0.8 - 0.6 * math.exp(-0.3 * layer)
            ao = _diff_attn(proj, diff_lambda, diff_subln, e, batch, seq, a_width, lam_init)
            bo = _dilated_attn(proj, batch, seq, 3 * a_width, b_width)
            xf, xg, ssq = _mm2_res(ao, bo, hy_w_out, e, xf, ffn_norm_next)
        else:
            o = layer // 2
            qd, kd = n_q_heads * HEAD_DIM, n_kv_heads * HEAD_DIM
            gains, flags = _section_gains([
                (qd, win_q_norm[o] * scale), (kd, win_k_norm[o]), (kd, None)])
            tn = _largest_tile(math.gcd(qd, kd), 1024)
            proj = _inproj(xg, ssq, win_w_in, o, gains, _tile_flags(flags, tn), cos, sin, tn)
            att = _window_attn(proj, win_sink, o, batch, seq, n_q_heads, n_kv_heads)
            xf, xg, ssq = _mm_res(att, win_w_out, o, xf, ffn_norm_next, 1024, "outproj")
        hidden = _gateup(xg, ssq, w_gate, w_up, layer)
        if layer + 1 < depth:
            xf, xg, ssq = _mm_res(hidden, w_down, layer, xf, (mix_gain, layer + 1), 512, "down")
        else:
            xf = _mm_res(hidden, w_down, layer, xf, None, 512, "down")
    return xf.reshape(batch, seq, d_model)
```

```python
import functools
import math

import jax
import jax.numpy as jnp
from jax import lax
from jax.experimental import pallas as pl
from jax.experimental.pallas import tpu as pltpu

HEAD_DIM = 128
DIL_PATTERNS = ((128, 1), (512, 4), (2048, 16))
WIN_HALF = 128
ROPE_THETA = 10000.0
EPS = 1e-6
NEG_INF = -1e30
LOG2_E = 1.4426950408889634

V7X_LANES = 128
V7X_MXU_COLS = 256
V7X_VMEM_LIMIT_CAP = 58 * 1024 * 1024

F32 = jnp.float32
BF16 = jnp.bfloat16
NT_DIMS = (((1,), (1,)), ((), ()))


def _params(n_grid_dims, vmem_bytes):
    return pltpu.CompilerParams(
        dimension_semantics=("arbitrary",) * n_grid_dims,
        vmem_limit_bytes=int(min(vmem_bytes, V7X_VMEM_LIMIT_CAP)))


def _largest_tile(n, cap, quantum=V7X_LANES):
    best = None
    t = quantum
    while t <= min(n, cap):
        if n % t == 0:
            best = t
        t += quantum
    assert best is not None, (n, cap)
    return best


def _lane_partial_sumsq(x):
    sq = x * x
    return functools.reduce(lambda a, b: a + b,
                            [sq[:, c:c + V7X_LANES] for c in range(0, x.shape[1], V7X_LANES)])


def _inv_rms(ssq_ref, rows, d_model):
    return lax.rsqrt(jnp.sum(ssq_ref[rows, :], axis=-1, keepdims=True) * (1.0 / d_model) + EPS)


ROW_PARTS = 8


def _row_parts(tm):
    part = tm // ROW_PARTS
    return [slice(r0, r0 + part) for r0 in range(0, tm, part)]


def _prenorm_kernel(x_ref, g_ref, xg_ref, ssq_ref):
    x = x_ref[...]
    xg_ref[...] = (x * g_ref[...]).astype(xg_ref.dtype)
    ssq_ref[...] = _lane_partial_sumsq(x)


def _prenorm(x, gains, layer):
    m, d = x.shape
    tm = _largest_tile(m, 256, 8)
    return pl.pallas_call(
        _prenorm_kernel,
        out_shape=[jax.ShapeDtypeStruct((m, d), BF16), jax.ShapeDtypeStruct((m, V7X_LANES), F32)],
        grid=(m // tm,),
        in_specs=[pl.BlockSpec((tm, d), lambda i: (i, 0)),
                  pl.BlockSpec((None, 1, d), lambda i: (layer, 0, 0))],
        out_specs=[pl.BlockSpec((tm, d), lambda i: (i, 0)),
                   pl.BlockSpec((tm, V7X_LANES), lambda i: (i, 0))],
        compiler_params=_params(1, 2 * tm * d * (4 + 2) + 4 * tm * d * 4),
        name="prenorm",
    )(x, gains)


def _emit_residual(res, rest, with_next):
    if not with_next:
        (o_ref,) = rest
        o_ref[...] = res
        return
    g_ref, o_ref, xg_ref, ssq_ref = rest
    o_ref[...] = res
    xg_ref[...] = (res * g_ref[...]).astype(xg_ref.dtype)
    part = _lane_partial_sumsq(res)
    j = pl.program_id(1)

    @pl.when(j == 0)
    def _():
        ssq_ref[...] = part

    @pl.when(j != 0)
    def _():
        ssq_ref[...] += part


def _residual_specs(m, n, tm, tn, next_norm):
    out_shape = [jax.ShapeDtypeStruct((m, n), F32)]
    out_specs = [pl.BlockSpec((tm, tn), lambda i, j: (i, j))]
    if next_norm is None:
        return [], [], out_shape[0], out_specs[0]
    gains, layer = next_norm
    out_shape += [jax.ShapeDtypeStruct((m, n), BF16), jax.ShapeDtypeStruct((m, V7X_LANES), F32)]
    out_specs += [pl.BlockSpec((tm, tn), lambda i, j: (i, j)),
                  pl.BlockSpec((tm, V7X_LANES), lambda i, j: (i, 0))]
    return [pl.BlockSpec((None, 1, tn), lambda i, j: (layer, 0, j))], [gains], out_shape, out_specs


def _norm_rope(x, g, cos, sin):
    ms = jnp.mean(x * x, axis=-1, keepdims=True)
    y = x * lax.rsqrt(ms + EPS) * g
    return y * cos + pltpu.roll(y, HEAD_DIM // 2, axis=1) * sin


def _inproj_kernel(flag_ref, a_ref, ssq_ref, b_ref, g_ref, cos_ref, sin_ref, o_ref):
    j = pl.program_id(1)
    tm, k = a_ref.shape

    @pl.when(flag_ref[j] == 1)
    def _():
        b = b_ref[...]
        accs = [(rows, jnp.dot(a_ref[rows, :], b, preferred_element_type=F32)) for rows in _row_parts(tm)]
        for rows, acc in accs:
            acc = acc * _inv_rms(ssq_ref, rows, k)
            cos = cos_ref[rows, :]
            sin = sin_ref[rows, :]
            for c in range(acc.shape[1] // HEAD_DIM):
                sl = slice(c * HEAD_DIM, (c + 1) * HEAD_DIM)
                o_ref[rows, sl] = _norm_rope(acc[:, sl], g_ref[:, sl], cos, sin).astype(o_ref.dtype)

    @pl.when(flag_ref[j] == 0)
    def _():
        acc = jnp.dot(a_ref[...], b_ref[...], preferred_element_type=F32)
        o_ref[...] = (acc * _inv_rms(ssq_ref, slice(None), k)).astype(o_ref.dtype)


def _inproj(xg, ssq, w, layer, gain_cols, flags, cos, sin, tn):
    m, k = xg.shape
    n = w.shape[2]
    s = cos.shape[0]
    tm = _largest_tile(s, 1024, 8)
    n_pos_blocks = s // tm
    vmem = 2 * (tm * k * 2 + k * tn * 2 + tm * tn * 2 + 2 * tm * HEAD_DIM * 4) + 3 * tm * tn * 4
    return pl.pallas_call(
        _inproj_kernel,
        out_shape=jax.ShapeDtypeStruct((m, n), BF16),
        grid_spec=pltpu.PrefetchScalarGridSpec(
            num_scalar_prefetch=1,
            grid=(m // tm, n // tn),
            in_specs=[pl.BlockSpec((tm, k), lambda i, j, f: (i, 0)),
                      pl.BlockSpec((tm, V7X_LANES), lambda i, j, f: (i, 0)),
                      pl.BlockSpec((None, k, tn), lambda i, j, f: (layer, 0, j)),
                      pl.BlockSpec((1, tn), lambda i, j, f: (0, j)),
                      pl.BlockSpec((tm, HEAD_DIM), lambda i, j, f: (i % n_pos_blocks, 0)),
                      pl.BlockSpec((tm, HEAD_DIM), lambda i, j, f: (i % n_pos_blocks, 0))],
            out_specs=pl.BlockSpec((tm, tn), lambda i, j, f: (i, j))),
        compiler_params=_params(2, vmem),
        name="inproj",
    )(flags, xg, ssq, w, gain_cols, cos, sin)


def _mm_res_kernel(a_ref, b_ref, r_ref, *rest, with_next):
    _emit_residual(r_ref[...] + jnp.dot(a_ref[...], b_ref[...], preferred_element_type=F32), rest, with_next)


def _mm_res(a, w, layer, r, next_norm, tm_cap, name, k_block=(0, 1)):
    kb, n_kb = k_block
    m = a.shape[0]
    k = a.shape[1] // n_kb
    n = w.shape[2]
    tm = _largest_tile(m, tm_cap, 8)
    tn = _largest_tile(n, 512)
    extra_specs, extra_ops, out_shape, out_specs = _residual_specs(m, n, tm, tn, next_norm)
    vmem = 2 * (tm * k * 2 + k * tn * 2 + 2 * tm * tn * 4 + tm * tn * 2 + tm * V7X_LANES * 4) + 4 * tm * tn * 4
    return pl.pallas_call(
        functools.partial(_mm_res_kernel, with_next=next_norm is not None),
        out_shape=out_shape,
        grid=(m // tm, n // tn),
        in_specs=[pl.BlockSpec((tm, k), lambda i, j: (i, kb)),
                  pl.BlockSpec((None, k, tn), lambda i, j: (layer, kb, j)),
                  pl.BlockSpec((tm, tn), lambda i, j: (i, j))] + extra_specs,
        out_specs=out_specs,
        compiler_params=_params(2, vmem),
        name=name,
    )(a, w, r, *extra_ops)


def _mm2_res_kernel(a1_ref, b1_ref, a2_ref, b2_ref, r_ref, *rest, with_next):
    acc = jnp.dot(a1_ref[...], b1_ref[...], preferred_element_type=F32)
    acc += jnp.dot(a2_ref[...], b2_ref[...], preferred_element_type=F32)
    _emit_residual(r_ref[...] + acc, rest, with_next)


def _mm2_res(a1, a2, w, layer, r, next_norm):
    m, k1 = a1.shape
    k2 = a2.shape[1]
    assert k1 == k2, "the two head groups index w[layer] as row blocks 0 and 1"
    n = w.shape[2]
    tm = _largest_tile(m, 1024, 8)
    tn = _largest_tile(n, 512)
    extra_specs, extra_ops, out_shape, out_specs = _residual_specs(m, n, tm, tn, next_norm)
    vmem = (2 * (tm * (k1 + k2) * 2 + (k1 + k2) * tn * 2 + 2 * tm * tn * 4 + tm * tn * 2 + tm * V7X_LANES * 4)
            + 4 * tm * tn * 4)
    return pl.pallas_call(
        functools.partial(_mm2_res_kernel, with_next=next_norm is not None),
        out_shape=out_shape,
        grid=(m // tm, n // tn),
        in_specs=[pl.BlockSpec((tm, k1), lambda i, j: (i, 0)),
                  pl.BlockSpec((None, k1, tn), lambda i, j: (layer, 0, j)),
                  pl.BlockSpec((tm, k2), lambda i, j: (i, 0)),
                  pl.BlockSpec((None, k2, tn), lambda i, j: (layer, 1, j)),
                  pl.BlockSpec((tm, tn), lambda i, j: (i, j))] + extra_specs,
        out_specs=out_specs,
        compiler_params=_params(2, vmem),
        name="outproj2",
    )(a1, w, a2, w, r, *extra_ops)


def _gateup_kernel(a_ref, ssq_ref, wg_ref, wu_ref, o_ref):
    wg = wg_ref[...].astype(BF16)
    wu = wu_ref[...].astype(BF16)
    parts = []
    for rows in _row_parts(a_ref.shape[0]):
        a = a_ref[rows, :]
        parts.append((rows, jnp.dot(a, wg, preferred_element_type=F32), jnp.dot(a, wu, preferred_element_type=F32)))
    for rows, g, u in parts:
        inv = _inv_rms(ssq_ref, rows, a_ref.shape[1])
        e = jnp.exp2(g * (inv * -LOG2_E))
        o_ref[rows, :] = ((g * u) * ((inv * inv) / (1.0 + e))).astype(o_ref.dtype)


def _gateup(xg, ssq, wg, wu, layer):
    m, k = xg.shape
    n = wg.shape[2]
    tm = _largest_tile(m, 2048, 8)
    tn = _largest_tile(n, 512, V7X_MXU_COLS)
    w_bytes = wg.dtype.itemsize
    vmem = 2 * (tm * k * 2 + 2 * k * tn * w_bytes + tm * tn * 2) + 2 * k * tn * 2 + 6 * tm * tn * 4
    return pl.pallas_call(
        _gateup_kernel,
        out_shape=jax.ShapeDtypeStruct((m, n), BF16),
        grid=(m // tm, n // tn),
        in_specs=[pl.BlockSpec((tm, k), lambda i, j: (i, 0)),
                  pl.BlockSpec((tm, V7X_LANES), lambda i, j: (i, 0)),
                  pl.BlockSpec((None, k, tn), lambda i, j: (layer, 0, j)),
                  pl.BlockSpec((None, k, tn), lambda i, j: (layer, 0, j))],
        out_specs=pl.BlockSpec((tm, tn), lambda i, j: (i, j)),
        compiler_params=_params(2, vmem),
        name="gateup",
    )(xg, ssq, wg, wu)


DIFF_ROW_CHUNK = 256


def _diff_attn_kernel(q_ref, k_ref, v_ref, lam_ref, subln_ref, o_ref, *, lam_init):
    q = q_ref[...]
    k = k_ref[...]
    v = v_ref[...]
    lf = lam_ref[...]
    lam = (jnp.exp(jnp.sum(lf[0:1] * lf[1:2], axis=-1, keepdims=True))
           - jnp.exp(jnp.sum(lf[2:3] * lf[3:4], axis=-1, keepdims=True)) + lam_init)

    halves = (slice(0, HEAD_DIM), slice(HEAD_DIM, 2 * HEAD_DIM))
    tq = q.shape[0]
    chunk = math.gcd(tq, DIFF_ROW_CHUNK)
    rows = [slice(r0, r0 + chunk) for r0 in range(0, tq, chunk)]

    def scores(rs):
        return [lax.dot_general(q[rs, c], k[:, c], NT_DIMS, preferred_element_type=F32) for c in halves]

    def finish(ss):
        ps = []
        for s in ss:
            p = jnp.exp2(s - jnp.max(s, axis=-1, keepdims=True))
            ps.append((p, jnp.sum(p, axis=-1, keepdims=True)))
        (p1, l1), (p2, l2) = ps
        a = p1 - p2 * (lam * l1 / l2)
        return jnp.dot(a.astype(BF16), v, preferred_element_type=F32) / l1

    pending = scores(rows[0])
    outs = []
    for nxt in rows[1:]:
        ahead = scores(nxt)
        outs.append(finish(pending))
        pending = ahead
    outs.append(finish(pending))
    o = jnp.concatenate(outs, axis=0)
    ms = jnp.mean(o * o, axis=-1, keepdims=True)
    y = o * lax.rsqrt(ms + EPS) * subln_ref[...]
    o_ref[...] = (y * (1.0 - lam_init)).astype(o_ref.dtype)


def _diff_attn(proj, lambdas, subln, layer, batch, seq, a_width, lam_init):
    m = proj.shape[0]
    hd2 = 2 * HEAD_DIM
    n_heads = a_width // hd2
    tq = _largest_tile(seq, 2048, 8)
    nq = seq // tq
    vmem = 2 * (2 * tq * hd2 * 2 + 2 * seq * hd2 * 2) + 6 * tq * seq * 4
    return pl.pallas_call(
        functools.partial(_diff_attn_kernel, lam_init=lam_init),
        out_shape=jax.ShapeDtypeStruct((m, a_width), BF16),
        grid=(batch, n_heads, nq),
        in_specs=[pl.BlockSpec((tq, hd2), lambda b, h, t: (b * nq + t, h)),
                  pl.BlockSpec((seq, hd2), lambda b, h, t: (b, n_heads + h)),
                  pl.BlockSpec((seq, hd2), lambda b, h, t: (b, 2 * n_heads + h)),
                  pl.BlockSpec((None, 4, HEAD_DIM), lambda b, h, t: (layer, 0, 0)),
                  pl.BlockSpec((None, 1, hd2), lambda b, h, t: (layer, 0, 0))],
        out_specs=pl.BlockSpec((tq, hd2), lambda b, h, t: (b * nq + t, h)),
        compiler_params=_params(3, vmem),
        name="diff_attn",
    )(proj, proj, proj, lambdas, subln)


BAND_TQ = 128


def _band_geometry(n, half_width):
    tq = min(BAND_TQ, n)
    kw = min(n, tq + 2 * half_width)
    return tq, kw, n // tq


def _band_mask(tq, kw, k0_minus_q0, half_width):
    rel = (lax.broadcasted_iota(jnp.int32, (tq, kw), 1) + k0_minus_q0
           - lax.broadcasted_iota(jnp.int32, (tq, kw), 0))
    return jnp.abs(rel) <= half_width


def _band_tiles(tiles, sinks=None):
    sinks = [None] * len(tiles) if sinks is None else sinks
    scores = [lax.dot_general(q, k, NT_DIMS, preferred_element_type=F32) for (q, k, _, _) in tiles]
    probs = []
    for s, (_, _, _, valid), sk in zip(scores, tiles, sinks):
        if isinstance(valid, (list, tuple)):
            width = s.shape[1] // len(valid)
            s = jnp.concatenate([blk if msk is None else jnp.where(msk, blk, NEG_INF) for blk, msk in
                                 ((s[:, c * width:(c + 1) * width], msk) for c, msk in enumerate(valid))], axis=1)
        else:
            s = jnp.where(valid, s, NEG_INF)
        m = jnp.max(s, axis=-1, keepdims=True)
        if sk is not None:
            m = jnp.maximum(m, sk)
        p = jnp.exp2(s - m)
        l = jnp.sum(p, axis=-1, keepdims=True)
        if sk is not None:
            l = l + jnp.exp2(sk - m)
        probs.append((p.astype(BF16), m, l))
    return [(jnp.dot(p, v, preferred_element_type=F32) / l, m + jnp.log2(l))
            for (p, m, l), (_, _, v, _) in zip(probs, tiles)]


def _fold(dst_ref, src_ref, factor, n_blocks, cast=None):
    n_tensors, rows = src_ref.shape[0], src_ref.shape[1]
    src_block = rows // n_blocks
    dst_block = src_block // factor
    for i in range(n_tensors):
        for j in range(n_blocks):
            for c in range(factor):
                v = src_ref[i, pl.ds(j * src_block + c, dst_block, stride=factor), :]
                dst_ref[i, (j * factor + c) * dst_block:(j * factor + c + 1) * dst_block, :] = (
                    v if cast is None else v.astype(cast))


def _unfold(dst_ref, src_ref, factor, n_blocks):
    n_tensors, rows = src_ref.shape[0], src_ref.shape[1]
    dst_block = rows // n_blocks
    src_block = dst_block // factor
    for i in range(n_tensors):
        for j in range(n_blocks):
            for c in range(factor):
                dst_ref[i, pl.ds(j * dst_block + c, src_block, stride=factor), :] = (
                    src_ref[i, (j * factor + c) * src_block:(j * factor + c + 1) * src_block, :])


def _dilated_kernel(q_ref, k_ref, v_ref, o_ref, nat_ref, x4_ref, x16_ref, r16_ref, u4_ref, r4_ref):
    seq = q_ref.shape[0]
    (w1, d1), (w4, d4), (w16, d16) = DIL_PATTERNS
    assert (d1, d4, d16) == (1, 4, 16) and w1 // (2 * d1) == w4 // (2 * d4) == w16 // (2 * d16)
    hw = w1 // 2

    for i, ref in enumerate((q_ref, k_ref, v_ref)):
        nat_ref[i] = ref[...].astype(F32)
    _fold(x4_ref, nat_ref, 4, 1)
    _fold(x16_ref, x4_ref, 4, 4, cast=BF16)

    sub16 = seq // 16
    tq, kw, n_tiles = _band_geometry(sub16, hw)
    assert n_tiles == 1 and kw == sub16
    valid16 = _band_mask(tq, kw, 0, hw)
    blocks = [slice(r * sub16, (r + 1) * sub16) for r in range(16)]
    tiles = [(x16_ref[0, rows, :], x16_ref[1, rows, :], x16_ref[2, rows, :], valid16) for rows in blocks]
    for rows, (o, lse) in zip(blocks, _band_tiles(tiles)):
        r16_ref[0, rows, :] = o
        r16_ref[1, rows, :] = jnp.broadcast_to(lse, (sub16, HEAD_DIM))
    _unfold(u4_ref, r16_ref, 4, 4)

    sub4 = seq // 4
    tq, kw, n_tiles = _band_geometry(sub4, hw)
    blocks, tiles = [], []
    for c in range(4):
        for t in range(n_tiles):
            q0 = t * tq
            k0 = min(max(q0 - hw, 0), sub4 - kw)
            qrows = slice(c * sub4 + q0, c * sub4 + q0 + tq)
            krows = slice(c * sub4 + k0, c * sub4 + k0 + kw)
            blocks.append(qrows)
            tiles.append((x4_ref[0, qrows, :].astype(BF16), x4_ref[1, krows, :].astype(BF16),
                          x4_ref[2, krows, :].astype(BF16), _band_mask(tq, kw, k0 - q0, hw)))
    for qrows, (o, lse) in zip(blocks, _band_tiles(tiles)):
        r4_ref[0, qrows, :] = o
        r4_ref[1, qrows, :] = jnp.broadcast_to(lse, (tq, HEAD_DIM))
    _unfold(r16_ref, u4_ref, 4, 1)
    _unfold(nat_ref, r4_ref, 4, 1)

    tq, kw, n_tiles = _band_geometry(seq, hw)
    blocks, tiles = [], []
    for t in range(n_tiles):
        q0 = t * tq
        k0 = min(max(q0 - hw, 0), seq - kw)
        qrows, krows = slice(q0, q0 + tq), slice(k0, k0 + kw)
        blocks.append(qrows)
        tiles.append((q_ref[qrows, :], k_ref[krows, :], v_ref[krows, :], _band_mask(tq, kw, k0 - q0, hw)))
    for qrows, (o, lse) in zip(blocks, _band_tiles(tiles)):
        lses = [lse, nat_ref[1, qrows, :], r16_ref[1, qrows, :]]
        outs = [o, nat_ref[0, qrows, :], r16_ref[0, qrows, :]]
        top = functools.reduce(jnp.maximum, lses)
        ws = [jnp.exp2(x - top) for x in lses]
        num = functools.reduce(lambda a, b: a + b, [w * x for w, x in zip(ws, outs)])
        den = functools.reduce(lambda a, b: a + b, ws)
        o_ref[qrows, :] = (num / den).astype(o_ref.dtype)


def _dilated_attn(proj, batch, seq, col0, b_width):
    m = proj.shape[0]
    n_heads = b_width // HEAD_DIM
    c0 = col0 // HEAD_DIM
    plane = seq * HEAD_DIM
    scratch_bytes = (3 + 3 + 2 + 2 + 2) * plane * 4 + 3 * plane * 2
    vmem = 2 * 4 * plane * 2 + scratch_bytes + 12 * 1024 * 1024
    spec = lambda off: pl.BlockSpec((seq, HEAD_DIM), lambda b, h: (b, c0 + off + h))
    return pl.pallas_call(
        _dilated_kernel,
        out_shape=jax.ShapeDtypeStruct((m, b_width), BF16),
        grid=(batch, n_heads),
        in_specs=[spec(0), spec(n_heads), spec(2 * n_heads)],
        out_specs=pl.BlockSpec((seq, HEAD_DIM), lambda b, h: (b, h)),
        scratch_shapes=[pltpu.VMEM((3, seq, HEAD_DIM), F32), pltpu.VMEM((3, seq, HEAD_DIM), F32),
                        pltpu.VMEM((3, seq, HEAD_DIM), BF16), pltpu.VMEM((2, seq, HEAD_DIM), F32),
                        pltpu.VMEM((2, seq, HEAD_DIM), F32), pltpu.VMEM((2, seq, HEAD_DIM), F32)],
        compiler_params=_params(2, vmem),
        name="dilated_attn",
    )(proj, proj, proj)


WIN_TILES_PER_STEP = 7


def _window_kernel(sink_ref, q_ref, k_ref, v_ref, o_ref, *, group, layer):
    seq = k_ref.shape[0]
    hk = pl.program_id(1)
    tq, kw, n_tiles = _band_geometry(seq, WIN_HALF)
    assert tq == WIN_HALF and kw == 3 * tq and (n_tiles - 2) % WIN_TILES_PER_STEP == 0, (seq, tq, kw)
    sinks = [sink_ref[layer, hk * group + g] * LOG2_E for g in range(group)]
    heads = [slice(g * HEAD_DIM, (g + 1) * HEAD_DIM) for g in range(group)]

    def run(tile_specs):
        dests, tiles = [], []
        for q0, k0, valid in tile_specs:
            qrows, krows = pl.ds(q0, tq), pl.ds(k0, kw)
            k = k_ref[krows, :]
            v = v_ref[krows, :]
            for cols in heads:
                dests.append((qrows, cols))
                tiles.append((q_ref[qrows, cols], k, v, valid))
        for (qrows, cols), (o, _) in zip(dests, _band_tiles(tiles, sinks * len(tile_specs))):
            o_ref[qrows, cols] = o.astype(o_ref.dtype)

    row = lax.broadcasted_iota(jnp.int32, (tq, tq), 0)
    col = lax.broadcasted_iota(jnp.int32, (tq, tq), 1)
    interior_valid = [col >= row, None, col <= row]

    def interior_body(step, carry):
        specs = []
        for tt in range(WIN_TILES_PER_STEP):
            q0 = pl.multiple_of((1 + step * WIN_TILES_PER_STEP + tt) * tq, tq)
            specs.append((q0, pl.multiple_of(q0 - tq, tq), interior_valid))
        run(specs)
        return carry

    lax.fori_loop(0, (n_tiles - 2) // WIN_TILES_PER_STEP, interior_body, 0)
    last = (n_tiles - 1) * tq
    run([(0, 0, _band_mask(tq, kw, 0, WIN_HALF)), (last, seq - kw, _band_mask(tq, kw, seq - kw - last, WIN_HALF))])


def _window_attn(proj, sinks, layer, batch, seq, n_q_heads, n_kv_heads):
    m = proj.shape[0]
    group = n_q_heads // n_kv_heads
    gw = group * HEAD_DIM
    vmem = 2 * (2 * seq * gw * 2 + 2 * seq * HEAD_DIM * 2) + 16 * 1024 * 1024
    return pl.pallas_call(
        functools.partial(_window_kernel, group=group, layer=layer),
        out_shape=jax.ShapeDtypeStruct((m, n_q_heads * HEAD_DIM), BF16),
        grid_spec=pltpu.PrefetchScalarGridSpec(
            num_scalar_prefetch=1,
            grid=(batch, n_kv_heads),
            in_specs=[pl.BlockSpec((seq, gw), lambda b, h, s: (b, h)),
                      pl.BlockSpec((seq, HEAD_DIM), lambda b, h, s: (b, n_q_heads + h)),
                      pl.BlockSpec((seq, HEAD_DIM), lambda b, h, s: (b, n_q_heads + n_kv_heads + h))],
            out_specs=pl.BlockSpec((seq, gw), lambda b, h, s: (b, h))),
        compiler_params=_params(2, vmem),
        name="window_attn",
    )(sinks, proj, proj, proj)


def _rope_tables(seq):
    inv_freq = ROPE_THETA ** (-jnp.arange(0, HEAD_DIM, 2, dtype=F32) / HEAD_DIM)
    ang = jnp.arange(seq, dtype=F32)[:, None] * inv_freq[None, :]
    ang = jnp.concatenate([ang, ang], axis=-1)
    sign = jnp.where(jnp.arange(HEAD_DIM) < HEAD_DIM // 2, -1.0, 1.0).astype(F32)
    return jnp.cos(ang), jnp.sin(ang) * sign


def _section_gains(sections):
    cols, flags = [], []
    for width, g in sections:
        reps = width // HEAD_DIM
        if g is None:
            cols.append(jnp.ones((width,), F32))
            flags += [0] * reps
        else:
            cols.append(jnp.tile(g.astype(F32), reps))
            flags += [1] * reps
    return jnp.concatenate(cols).reshape(1, -1), flags


def _tile_flags(flags, tn):
    per = tn // HEAD_DIM
    tiles = [flags[i:i + per] for i in range(0, len(flags), per)]
    assert all(len(set(t)) == 1 for t in tiles), "a column tile mixes normed and plain heads"
    return jnp.asarray([t[0] for t in tiles], jnp.int32)


def kernel(x, mix_norm, ffn_norm, w_gate, w_up, w_down, hy_w_in, hy_w_out, diff_q_norm, diff_k_norm,
           diff_lambda, diff_subln, dil_q_norm, dil_k_norm, win_w_in, win_w_out, win_q_norm, win_k_norm,
           win_sink):
    batch, seq, d_model = x.shape
    depth = mix_norm.shape[0]
    a_width = d_model // 2
    b_width = d_model - a_width
    n_q_heads = d_model // HEAD_DIM
    n_kv_heads = (win_w_in.shape[-1] // HEAD_DIM - n_q_heads) // 2
    scale = HEAD_DIM ** -0.5 * LOG2_E

    cos, sin = _rope_tables(seq)
    xf = x.reshape(batch * seq, d_model)
    mix_gain = mix_norm.astype(F32)[:, None, :]
    ffn_gain = ffn_norm.astype(F32)[:, None, :]
    w_down = w_down.astype(BF16)
    hy_w_in, hy_w_out, win_w_in, win_w_out = (w.astype(BF16) for w in (hy_w_in, hy_w_out, win_w_in, win_w_out))
    diff_lambda = diff_lambda.astype(F32)
    diff_subln = diff_subln.astype(F32)[:, None, :]
    win_sink = win_sink.astype(F32)

    xg, ssq = _prenorm(xf, mix_gain, 0)
    for layer in range(depth):
        ffn_norm_next = (ffn_gain, layer)
        if layer % 2 == 0:
            e = layer // 2
            gains, flags = _section_gains([
                (a_width, diff_q_norm[e] * scale), (a_width, diff_k_norm[e]), (a_width, None),
                (b_width, dil_q_norm[e] * scale), (b_width, dil_k_norm[e]), (b_width, None)])
            tn = _largest_tile(math.gcd(a_width, b_width), 1024)
            proj = _inproj(xg, ssq, hy_w_in, e, gains, _tile_flags(flags, tn), cos, sin, tn)
            lam_init = 0.8 - 0.6 * math.exp(-0.3 * layer)
            ao = _diff_attn(proj, diff_lambda, diff_subln, e, batch, seq, a_width, lam_init)
            bo = _dilated_attn(proj, batch, seq, 3 * a_width, b_width)
            xf, xg, ssq = _mm2_res(ao, bo, hy_w_out, e, xf, ffn_norm_next)
        else:
            o = layer // 2
            qd, kd = n_q_heads * HEAD_DIM, n_kv_heads * HEAD_DIM
            gains, flags = _section_gains([
                (qd, win_q_norm[o] * scale), (kd, win_k_norm[o]), (kd, None)])
            tn = _largest_tile(math.gcd(qd, kd), 1024)
            proj = _inproj(xg, ssq, win_w_in, o, gains, _tile_flags(flags, tn), cos, sin, tn)
            att = _window_attn(proj, win_sink, o, batch, seq, n_q_heads, n_kv_heads)
            xf, xg, ssq = _mm_res(att, win_w_out, o, xf, ffn_norm_next, 1024, "outproj")
        hidden = _gateup(xg, ssq, w_gate, w_up, layer)
        xf = _mm_res(hidden, w_down, layer, xf, None, 1024, "down", (0, 2))
        if layer + 1 < depth:
            xf, xg, ssq = _mm_res(hidden, w_down, layer, xf, (mix_gain, layer + 1), 1024, "down", (1, 2))
        else:
            xf = _mm_res(hidden, w_down, layer, xf, None, 1024, "down", (1, 2))
    return xf.reshape(batch, seq, d_model)
```
